```python
import math
import jax, jax.numpy as jnp
from jax import lax
import numpy as np

D_MODEL = 1024
BATCH = 4
SEQ = 4096
DEPTH = 4

CHUNK = 64
N_MIXERS = 3
N_RET = (DEPTH + 2) // 3
N_CONV = (DEPTH + 1) // 3
N_SB = DEPTH // 3
RET_HEADS = 4
RET_DK = D_MODEL // RET_HEADS
RET_DV = 2 * D_MODEL // RET_HEADS
ROPE_BASE = 10000.0
CONV_WIDTH = 31
SB_HEADS = 16
SB_DH = D_MODEL // SB_HEADS
SB_BLOCK = 128
D_FF = 4 * D_MODEL
EPS = 1e-6

kernel_name = "hybrid_ret_conv_stickbreak_trunk"


def rms_norm(x, w):
    xf = x.astype(jnp.float32)
    y = xf * lax.rsqrt(jnp.mean(xf * xf, axis=-1, keepdims=True) + EPS)
    return (y * w.astype(jnp.float32)).astype(x.dtype)


def layer_norm(x, w, b):
    xf = x.astype(jnp.float32)
    mu = jnp.mean(xf, axis=-1, keepdims=True)
    var = jnp.mean(jnp.square(xf - mu), axis=-1, keepdims=True)
    y = (xf - mu) * lax.rsqrt(var + EPS)
    return y * w.astype(jnp.float32) + b.astype(jnp.float32)


def apply_rotary(t, seq_len):
    half = t.shape[-1] // 2
    inv_freq = ROPE_BASE ** (-jnp.arange(half, dtype=jnp.float32) / half)
    ang = jnp.arange(seq_len, dtype=jnp.float32)[:, None] * inv_freq[None, :]
    cos = jnp.cos(ang)[:, None, :]
    sin = jnp.sin(ang)[:, None, :]
    t1, t2 = t[..., :half], t[..., half:]
    return jnp.concatenate([t1 * cos - t2 * sin, t1 * sin + t2 * cos], axis=-1)


def retention_chunkwise(q, k, v):
    b, s, h, _ = q.shape
    n = s // CHUNK

    def to_chunks(t):
        return t.reshape(b, n, CHUNK, h, -1).transpose(1, 0, 3, 2, 4)

    log_g = jnp.log(1.0 - jnp.exp2(-5.0 - jnp.arange(h, dtype=jnp.float32)))
    idx = jnp.arange(CHUNK, dtype=jnp.float32)
    intra_decay = jnp.exp(log_g[:, None, None] * jnp.abs(idx[:, None] - idx[None, :]))
    q_decay = jnp.exp(log_g[:, None] * (idx + 1.0))[..., None]
    k_decay = jnp.exp(log_g[:, None] * (CHUNK - 1.0 - idx))[..., None]
    chunk_decay = jnp.exp(log_g * CHUNK)[:, None, None]

    def step(state, inp):
        qc, kc, vc = inp
        scores = jnp.einsum('bhcd,bhsd->bhcs', qc, kc) * intra_decay
        out = (jnp.einsum('bhcs,bhse->bhce', scores, vc)
               + jnp.einsum('bhcd,bhde->bhce', qc * q_decay, state))
        state = state * chunk_decay + jnp.einsum('bhsd,bhse->bhde', kc * k_decay, vc)
        return state, out

    state0 = jnp.zeros((b, h, q.shape[-1], v.shape[-1]), jnp.float32)
    _, out = lax.scan(step, state0, (to_chunks(q), to_chunks(k), to_chunks(v)))
    return out.transpose(1, 0, 3, 2, 4).reshape(b, s, h, -1)


def retention_mixer(xn, w_in, q_gain, k_gain, gn_w, gn_b, w_out):
    b, s, _ = xn.shape
    proj = xn @ w_in
    q, k, v, g = jnp.split(proj, [D_MODEL, 2 * D_MODEL, 4 * D_MODEL], axis=-1)
    q = rms_norm(q.reshape(b, s, RET_HEADS, RET_DK), q_gain).astype(jnp.float32)
    k = rms_norm(k.reshape(b, s, RET_HEADS, RET_DK), k_gain).astype(jnp.float32)
    v = v.reshape(b, s, RET_HEADS, RET_DV).astype(jnp.float32)
    q = apply_rotary(q, s)
    k = apply_rotary(k, s) * (RET_DK ** -0.5)
    y = retention_chunkwise(q, k, v)
    mu = jnp.mean(y, axis=-1, keepdims=True)
    var = jnp.mean(jnp.square(y - mu), axis=-1, keepdims=True)
    y = (y - mu) * lax.rsqrt(var + EPS)
    y = y * gn_w.astype(jnp.float32).reshape(RET_HEADS, RET_DV) + gn_b.astype(jnp.float32).reshape(RET_HEADS, RET_DV)
    y = y.reshape(b, s, 2 * D_MODEL).astype(xn.dtype)
    return (jax.nn.silu(g) * y) @ w_out


def conformer_conv_mixer(xn, pw1_w, pw1_b, dw_w, dw_b, ln_w, ln_b, pw2_w, pw2_b):
    h = xn @ pw1_w + pw1_b
    a, gate = jnp.split(h, 2, axis=-1)
    h = a * jax.nn.sigmoid(gate)
    h = lax.conv_general_dilated(
        h, dw_w[:, None, :].astype(h.dtype), window_strides=(1,),
        padding=[(CONV_WIDTH - 1, 0)],
        dimension_numbers=('NWC', 'WIO', 'NWC'),
        feature_group_count=D_MODEL) + dw_b
    h = layer_norm(h, ln_w, ln_b)
    h = jax.nn.silu(h).astype(xn.dtype)
    return h @ pw2_w + pw2_b


def stick_breaking_mixer(xn, w_in, q_gain, k_gain, w_out):
    b, s, _ = xn.shape
    proj = xn @ w_in
    q, k, v = jnp.split(proj, 3, axis=-1)
    q = rms_norm(q.reshape(b, s, SB_HEADS, SB_DH), q_gain).astype(jnp.float32).transpose(0, 2, 1, 3)
    k = rms_norm(k.reshape(b, s, SB_HEADS, SB_DH), k_gain).astype(jnp.float32).transpose(0, 2, 1, 3)
    v = v.reshape(b, s, SB_HEADS, SB_DH).astype(jnp.float32).transpose(0, 2, 1, 3)
    scale = SB_DH ** -0.5
    outs = []
    for b0 in range(0, s, SB_BLOCK):
        kend = b0 + SB_BLOCK
        z = jnp.einsum('bhqd,bhkd->bhqk', q[:, :, b0:kend], k[:, :, :kend]) * scale
        t_idx = b0 + jnp.arange(SB_BLOCK)
        s_idx = jnp.arange(kend)
        mask = s_idx[None, :] < t_idx[:, None]
        log_1mb = jnp.where(mask, jax.nn.log_sigmoid(-z), 0.0)
        log_w = jax.nn.log_sigmoid(z) + lax.cumsum(log_1mb, axis=3, reverse=True) - log_1mb
        a = jnp.where(mask, jnp.exp(log_w), 0.0)
        outs.append(jnp.einsum('bhqk,bhkd->bhqd', a, v[:, :, :kend]))
    y = jnp.concatenate(outs, axis=2).transpose(0, 2, 1, 3).reshape(b, s, D_MODEL).astype(xn.dtype)
    return y @ w_out


def squared_relu_mlp(xn, w1, w2):
    h = jax.nn.relu(xn @ w1)
    return (h * h) @ w2


def setup_inputs(seed: int = 0) -> dict:
    key = jax.random.key(seed)
    ks = iter(jax.random.split(key, 32))
    f32 = jnp.float32

    def nrm(shape, scale):
        return jax.random.normal(next(ks), shape, f32) * scale

    def gain(shape):
        return 1.0 + 0.02 * jax.random.normal(next(ks), shape, f32)

    D = D_MODEL
    return {
        "x": jax.random.normal(next(ks), (BATCH, SEQ, D), f32),
        "norm_mix": gain((DEPTH, D)),
        "norm_ffn": gain((DEPTH, D)),
        "ret_w_in": nrm((N_RET, D, 6 * D), D ** -0.5),
        "ret_q_norm": gain((N_RET, RET_DK)),
        "ret_k_norm": gain((N_RET, RET_DK)),
        "ret_gn_w": gain((N_RET, 2 * D)),
        "ret_gn_b": nrm((N_RET, 2 * D), 0.01),
        "ret_w_out": nrm((N_RET, 2 * D, D), (2 * D) ** -0.5),
        "conv_pw1_w": nrm((N_CONV, D, 2 * D), D ** -0.5),
        "conv_pw1_b": nrm((N_CONV, 2 * D), 0.01),
        "conv_dw_w": nrm((N_CONV, CONV_WIDTH, D), CONV_WIDTH ** -0.5),
        "conv_dw_b": nrm((N_CONV, D), 0.01),
        "conv_ln_w": gain((N_CONV, D)),
        "conv_ln_b": nrm((N_CONV, D), 0.01),
        "conv_pw2_w": nrm((N_CONV, D, D), D ** -0.5),
        "conv_pw2_b": nrm((N_CONV, D), 0.01),
        "sb_w_in": nrm((N_SB, D, 3 * D), D ** -0.5),
        "sb_q_norm": gain((N_SB, SB_DH)),
        "sb_k_norm": gain((N_SB, SB_DH)),
        "sb_w_out": nrm((N_SB, D, D), D ** -0.5),
        "ffn_w1": nrm((DEPTH, D, D_FF), D ** -0.5),
        "ffn_w2": nrm((DEPTH, D_FF, D), D_FF ** -0.5),
        "final_norm": gain((D,)),
    }


def reference(x, norm_mix, norm_ffn,
              ret_w_in, ret_q_norm, ret_k_norm, ret_gn_w, ret_gn_b, ret_w_out,
              conv_pw1_w, conv_pw1_b, conv_dw_w, conv_dw_b, conv_ln_w, conv_ln_b, conv_pw2_w, conv_pw2_b,
              sb_w_in, sb_q_norm, sb_k_norm, sb_w_out,
              ffn_w1, ffn_w2, final_norm):
    for i in range(DEPTH):
        kind = i % N_MIXERS
        j = i // N_MIXERS
        h = rms_norm(x, norm_mix[i])
        if kind == 0:
            m = retention_mixer(h, ret_w_in[j], ret_q_norm[j], ret_k_norm[j],
                                ret_gn_w[j], ret_gn_b[j], ret_w_out[j])
        elif kind == 1:
            m = conformer_conv_mixer(h, conv_pw1_w[j], conv_pw1_b[j], conv_dw_w[j], conv_dw_b[j],
                                     conv_ln_w[j], conv_ln_b[j], conv_pw2_w[j], conv_pw2_b[j])
        else:
            m = stick_breaking_mixer(h, sb_w_in[j], sb_q_norm[j], sb_k_norm[j], sb_w_out[j])
        x = x + m
        x = x + squared_relu_mlp(rms_norm(x, norm_ffn[i]), ffn_w1[i], ffn_w2[i])
    return rms_norm(x, final_norm)
```

```python
import functools

import jax
import jax.numpy as jnp
from jax import lax
from jax.experimental import pallas as pl
from jax.experimental.pallas import tpu as pltpu

F32 = jnp.float32
BF16 = jnp.bfloat16
EPS = 1e-6

CHUNK = 64
RET_HEADS = 4
ROPE_BASE = 10000.0
CONV_WIDTH = 31
SB_HEADS = 16
N_MIXERS = 3

V7X_VMEM_LIMIT_BYTES = 56 * 1024 * 1024
CONV_HALO = 32


def _cparams(n_axes):
    return pltpu.CompilerParams(
        dimension_semantics=("arbitrary",) * n_axes,
        vmem_limit_bytes=V7X_VMEM_LIMIT_BYTES)


def _resident(shape):
    zeros = (0,) * len(shape)
    return pl.BlockSpec(shape, lambda *_: zeros, pipeline_mode=pl.Buffered(1))


def _rms(x, g):
    return x * lax.rsqrt(jnp.mean(x * x, axis=-1, keepdims=True) + EPS) * g


def _dot(a, b):
    return jnp.dot(a, b, preferred_element_type=F32)


def _dot_nt(a, b):
    return lax.dot_general(a, b, (((1,), (1,)), ((), ())), preferred_element_type=F32)


def _dot_tn(a, b):
    return lax.dot_general(a, b, (((0,), (0,)), ((), ())), preferred_element_type=F32)


def _norm_proj_kernel(x_ref, nw_ref, w_ref, o_ref, *, chunk):
    xn = _rms(x_ref[...], nw_ref[...]).astype(BF16)
    for c in range(0, o_ref.shape[1], chunk):
        o_ref[:, c:c + chunk] = _dot(xn, w_ref[:, c:c + chunk]).astype(o_ref.dtype)


def norm_proj(x, nw, w, *, tm, out_dtype=F32, chunk=512):
    t, d = x.shape
    n = w.shape[1]
    return pl.pallas_call(
        functools.partial(_norm_proj_kernel, chunk=chunk),
        grid=(t // tm,),
        in_specs=[pl.BlockSpec((tm, d), lambda i: (i, 0)),
                  _resident((1, d)), _resident((d, n))],
        out_specs=pl.BlockSpec((tm, n), lambda i: (i, 0)),
        out_shape=jax.ShapeDtypeStruct((t, n), out_dtype),
        compiler_params=_cparams(1),
        name="norm_proj",
    )(x, nw, w)


def _norm_glu_kernel(x_ref, nw_ref, w_ref, b_ref, o_ref, *, chunk):
    d = o_ref.shape[1]
    xn = _rms(x_ref[...], nw_ref[...]).astype(BF16)
    for c in range(0, d, chunk):
        a = _dot(xn, w_ref[:, c:c + chunk]) + b_ref[:, c:c + chunk]
        gate = _dot(xn, w_ref[:, d + c:d + c + chunk]) + b_ref[:, d + c:d + c + chunk]
        o_ref[:, c:c + chunk] = a * jax.nn.sigmoid(gate)


def norm_glu(x, nw, w, b, *, tm, chunk=512):
    t, d = x.shape
    return pl.pallas_call(
        functools.partial(_norm_glu_kernel, chunk=chunk),
        grid=(t // tm,),
        in_specs=[pl.BlockSpec((tm, d), lambda i: (i, 0)),
                  _resident((1, d)), _resident((d, 2 * d)), _resident((1, 2 * d))],
        out_specs=pl.BlockSpec((tm, d), lambda i: (i, 0)),
        out_shape=jax.ShapeDtypeStruct((t, d), F32),
        compiler_params=_cparams(1),
        name="norm_glu",
    )(x, nw, w, b)


def _mm_res_kernel(z_ref, w_ref, b_ref, x_ref, o_ref):
    o_ref[...] = x_ref[...] + _dot(z_ref[...], w_ref[...]) + b_ref[...]


def mm_res(z, w, b, x, *, tm):
    t, kdim = z.shape
    d = w.shape[1]
    return pl.pallas_call(
        _mm_res_kernel,
        grid=(t // tm,),
        in_specs=[pl.BlockSpec((tm, kdim), lambda i: (i, 0)),
                  _resident((kdim, d)), _resident((1, d)),
                  pl.BlockSpec((tm, d), lambda i: (i, 0))],
        out_specs=pl.BlockSpec((tm, d), lambda i: (i, 0)),
        out_shape=jax.ShapeDtypeStruct((t, d), F32),
        compiler_params=_cparams(1),
        name="mm_res",
    )(z, w, b, x)


def _ffn_kernel(x_ref, nw_ref, w1_ref, w2_ref, fw_ref, o_ref, h_scr, *, chunk, final_norm):
    x = x_ref[...]
    xn = _rms(x, nw_ref[...]).astype(BF16)
    for c in range(0, h_scr.shape[1], chunk):
        h = jnp.maximum(_dot(xn, w1_ref[:, c:c + chunk]), 0.0)
        h_scr[:, c:c + chunk] = (h * h).astype(BF16)
    y = x + _dot(h_scr[...], w2_ref[...])
    if final_norm:
        y = _rms(y, fw_ref[...])
    o_ref[...] = y


def ffn(x, nw, w1, w2, fw, *, tm, final_norm, chunk=1024):
    t, d = x.shape
    dff = w1.shape[1]
    return pl.pallas_call(
        functools.partial(_ffn_kernel, chunk=chunk, final_norm=final_norm),
        grid=(t // tm,),
        in_specs=[pl.BlockSpec((tm, d), lambda i: (i, 0)),
                  _resident((1, d)), _resident((d, dff)), _resident((dff, d)),
                  _resident((1, d))],
        out_specs=pl.BlockSpec((tm, d), lambda i: (i, 0)),
        out_shape=jax.ShapeDtypeStruct((t, d), F32),
        scratch_shapes=[pltpu.VMEM((tm, dff), BF16)],
        compiler_params=_cparams(1),
        name="ffn",
    )(x, nw, w1, w2, fw)


def _ret_proj_kernel(x_ref, nw_ref, w_ref, qg_ref, kg_ref, cos_ref, sin_ref,
                     q_ref, k_ref, v_ref, g_ref, *, heads, chunk):
    d = q_ref.shape[1]
    dk = d // heads
    half = dk // 2
    xn = _rms(x_ref[...], nw_ref[...]).astype(BF16)
    cos = cos_ref[...]
    sin = sin_ref[...]
    for base, gain_ref, o_ref, scale in ((0, qg_ref, q_ref, 1.0), (d, kg_ref, k_ref, dk ** -0.5)):
        for h in range(heads):
            y = _rms(_dot(xn, w_ref[:, base + h * dk:base + (h + 1) * dk]), gain_ref[...])
            t1 = y[:, :half]
            t2 = y[:, half:]
            o_ref[:, h * dk:h * dk + half] = ((t1 * cos - t2 * sin) * scale).astype(BF16)
            o_ref[:, h * dk + half:(h + 1) * dk] = ((t1 * sin + t2 * cos) * scale).astype(BF16)
    for c in range(0, 2 * d, chunk):
        v_ref[:, c:c + chunk] = _dot(xn, w_ref[:, 2 * d + c:2 * d + c + chunk]).astype(BF16)
        g_ref[:, c:c + chunk] = _dot(xn, w_ref[:, 4 * d + c:4 * d + c + chunk])


def ret_proj(x, nw, w, qg, kg, cos, sin, *, tm, seq, chunk=512):
    t, d = x.shape
    spb = seq // tm
    row = lambda i: (i, 0)
    return pl.pallas_call(
        functools.partial(_ret_proj_kernel, heads=RET_HEADS, chunk=chunk),
        grid=(t // tm,),
        in_specs=[pl.BlockSpec((tm, d), row),
                  _resident((1, d)), _resident((d, 6 * d)),
                  _resident(qg.shape), _resident(kg.shape),
                  pl.BlockSpec((tm, cos.shape[1]), lambda i: (i % spb, 0)),
                  pl.BlockSpec((tm, sin.shape[1]), lambda i: (i % spb, 0))],
        out_specs=[pl.BlockSpec((tm, d), row), pl.BlockSpec((tm, d), row),
                   pl.BlockSpec((tm, 2 * d), row), pl.BlockSpec((tm, 2 * d), row)],
        out_shape=[jax.ShapeDtypeStruct((t, d), BF16), jax.ShapeDtypeStruct((t, d), BF16),
                   jax.ShapeDtypeStruct((t, 2 * d), BF16), jax.ShapeDtypeStruct((t, 2 * d), F32)],
        compiler_params=_cparams(1),
        name="ret_proj",
    )(x, nw, w, qg, kg, cos, sin)


def _ret_core_kernel(q_ref, k_ref, v_ref, g_ref, dm_ref, qd_ref, kd_ref, cd_ref,
                     gw_ref, gb_ref, z_ref, state):
    @pl.when(pl.program_id(2) == 0)
    def _():
        state[...] = jnp.zeros_like(state)

    q = q_ref[0]
    k = k_ref[0]
    v = v_ref[0]
    st = state[...]
    p = (_dot_nt(q, k) * dm_ref[0]).astype(BF16)
    y = _dot(p, v) + qd_ref[0] * _dot(q, st.astype(BF16))
    kk = (k.astype(F32) * kd_ref[0]).astype(BF16)
    state[...] = st * cd_ref[0] + _dot_tn(kk, v)

    mu = jnp.mean(y, axis=-1, keepdims=True)
    yc = y - mu
    var = jnp.mean(yc * yc, axis=-1, keepdims=True)
    yn = yc * lax.rsqrt(var + EPS) * gw_ref[...] + gb_ref[...]
    g = g_ref[0]
    z_ref[0] = (g * jax.nn.sigmoid(g) * yn).astype(BF16)


def _ret_decay_tables(blk):
    h = jnp.arange(RET_HEADS, dtype=F32)
    log_g = jnp.log(1.0 - jnp.exp2(-5.0 - h))
    idx = jnp.arange(blk, dtype=F32)
    dist = idx[:, None] - idx[None, :]
    ct = jnp.arange(blk)[:, None] // CHUNK
    cs = jnp.arange(blk)[None, :] // CHUNK
    expo = jnp.where(ct == cs, jnp.abs(dist), dist)
    dm = jnp.where((cs <= ct)[None], jnp.exp(log_g[:, None, None] * expo[None]), 0.0)
    qd = jnp.exp(log_g[:, None] * (idx + 1.0))[..., None]
    kd = jnp.exp(log_g[:, None] * (blk - 1.0 - idx))[..., None]
    cd = jnp.exp(log_g * blk)[:, None, None]
    return dm, qd, kd, cd


def ret_core(q, k, v, g, gn_w, gn_b, *, blk):
    b, s, d = q.shape
    dk = d // RET_HEADS
    dv = 2 * d // RET_HEADS
    dm, qd, kd, cd = _ret_decay_tables(blk)
    tok = lambda bi, h, n: (bi, n, h)
    per_head = lambda bi, h, n: (h, 0, 0)
    return pl.pallas_call(
        _ret_core_kernel,
        grid=(b, RET_HEADS, s // blk),
        in_specs=[pl.BlockSpec((1, blk, dk), tok), pl.BlockSpec((1, blk, dk), tok),
                  pl.BlockSpec((1, blk, dv), tok), pl.BlockSpec((1, blk, dv), tok),
                  pl.BlockSpec((1, blk, blk), per_head),
                  pl.BlockSpec((1, blk, 1), per_head), pl.BlockSpec((1, blk, 1), per_head),
                  pl.BlockSpec((1, 1, 1), per_head),
                  pl.BlockSpec((1, dv), lambda bi, h, n: (0, h)),
                  pl.BlockSpec((1, dv), lambda bi, h, n: (0, h))],
        out_specs=pl.BlockSpec((1, blk, dv), tok),
        out_shape=jax.ShapeDtypeStruct((b, s, 2 * d), BF16),
        scratch_shapes=[pltpu.VMEM((dk, dv), F32)],
        compiler_params=_cparams(3),
        name="ret_core",
    )(q, k, v, g, dm, qd, kd, cd, gn_w, gn_b)


def _conv_tail_kernel(u_ref, halo_ref, dw_ref, dwb_ref, lw_ref, lb_ref, w2_ref, b2_ref, x_ref,
                      o_ref, pad_scr, conv_scr, *, rows):
    tc = u_ref.shape[1]
    width = dw_ref.shape[0]
    halo = halo_ref[0]
    pad_scr[0:CONV_HALO, :] = jnp.where(pl.program_id(1) > 0, halo, jnp.zeros_like(halo))
    pad_scr[CONV_HALO:CONV_HALO + tc, :] = u_ref[0]
    first = CONV_HALO - (width - 1)
    for r0 in range(0, tc, rows):
        acc = dw_ref[0:1, :] * pad_scr[first + r0:first + r0 + rows, :]
        for j in range(1, width):
            acc = acc + dw_ref[j:j + 1, :] * pad_scr[first + r0 + j:first + r0 + j + rows, :]
        conv_scr[r0:r0 + rows, :] = acc
    hc = conv_scr[...] + dwb_ref[...]
    mu = jnp.mean(hc, axis=-1, keepdims=True)
    c = hc - mu
    var = jnp.mean(c * c, axis=-1, keepdims=True)
    hn = c * lax.rsqrt(var + EPS) * lw_ref[...] + lb_ref[...]
    act = (hn * jax.nn.sigmoid(hn)).astype(BF16)
    o_ref[0] = x_ref[0] + _dot(act, w2_ref[...]) + b2_ref[...]


def conv_tail(u, dw_w, dw_b, ln_w, ln_b, w2, b2, x, *, tc, rows=32):
    b, s, d = u.shape
    per_tile = tc // CONV_HALO
    tok = lambda bi, n: (bi, n, 0)
    return pl.pallas_call(
        functools.partial(_conv_tail_kernel, rows=rows),
        grid=(b, s // tc),
        in_specs=[pl.BlockSpec((1, tc, d), tok),
                  pl.BlockSpec((1, CONV_HALO, d),
                               lambda bi, n: (bi, jnp.maximum(n * per_tile - 1, 0), 0)),
                  _resident(dw_w.shape), _resident((1, d)), _resident((1, d)), _resident((1, d)),
                  _resident((d, d)), _resident((1, d)),
                  pl.BlockSpec((1, tc, d), tok)],
        out_specs=pl.BlockSpec((1, tc, d), tok),
        out_shape=jax.ShapeDtypeStruct((b, s, d), F32),
        scratch_shapes=[pltpu.VMEM((CONV_HALO + tc, d), F32), pltpu.VMEM((tc, d), F32)],
        compiler_params=_cparams(2),
        name="conv_tail",
    )(u, u, dw_w, dw_b, ln_w, ln_b, w2, b2, x)


def _softplus(z):
    return jnp.maximum(z, 0.0) + jnp.log(1.0 + jnp.exp(-jnp.abs(z)))


def _suffix_sum(l, u):
    hi = l.astype(BF16)
    lo = (l - hi.astype(F32)).astype(BF16)
    return _dot(hi, u) + _dot(lo, u)


def _sb_attn_kernel(q_ref, k_ref, v_ref, qg_ref, kg_ref, u_ref, o_ref, kn_scr, vn_scr, *, dh):
    i = pl.program_id(2)
    tq = q_ref.shape[1]
    pair = 128 // dh

    @pl.when(i == 0)
    def _():
        for e in range(pair):
            kn_scr[e] = _rms(k_ref[0, :, e * dh:(e + 1) * dh], kg_ref[...]).astype(BF16)
            vn_scr[e] = v_ref[0, :, e * dh:(e + 1) * dh].astype(BF16)

    u = u_ref[...]
    row = lax.broadcasted_iota(jnp.int32, (tq, tq), 0)
    col = lax.broadcasted_iota(jnp.int32, (tq, tq), 1)
    causal = col < row
    outs = []
    for e in range(pair):
        q = (_rms(q_ref[0, :, e * dh:(e + 1) * dh], qg_ref[...]) * dh ** -0.5).astype(BF16)

        start = pl.multiple_of(i * tq, tq)
        z = _dot_nt(q, kn_scr[e, pl.ds(start, tq), :])
        sp = _softplus(z)
        l = jnp.where(causal, sp, 0.0)
        r = _suffix_sum(l, u)
        a = jnp.where(causal, jnp.exp(z - sp - r), 0.0)
        acc = _dot(a.astype(BF16), vn_scr[e, pl.ds(start, tq), :])
        carry = r[:, 0:1] + l[:, 0:1]

        def body(step, state):
            acc, carry = state
            start = pl.multiple_of((i - 1 - step) * tq, tq)
            z = _dot_nt(q, kn_scr[e, pl.ds(start, tq), :])
            sp = _softplus(z)
            r = _suffix_sum(sp, u)
            a = jnp.exp(z - sp - r - carry)
            acc = acc + _dot(a.astype(BF16), vn_scr[e, pl.ds(start, tq), :])
            return acc, carry + r[:, 0:1] + sp[:, 0:1]

        acc, _ = lax.fori_loop(0, i, body, (acc, carry))
        outs.append(acc)
    o_ref[0] = jnp.concatenate(outs, axis=-1).astype(o_ref.dtype)


def sb_attention(proj, qg, kg, *, tq):
    b, s, d3 = proj.shape
    d = d3 // 3
    dh = d // SB_HEADS
    lane_blocks = d // 128
    idx = jnp.arange(tq)
    u = (idx[:, None] > idx[None, :]).astype(BF16)
    return pl.pallas_call(
        functools.partial(_sb_attn_kernel, dh=dh),
        grid=(b, lane_blocks, s // tq),
        in_specs=[pl.BlockSpec((1, tq, 128), lambda bi, hp, i: (bi, i, hp)),
                  pl.BlockSpec((1, s, 128), lambda bi, hp, i: (bi, 0, lane_blocks + hp)),
                  pl.BlockSpec((1, s, 128), lambda bi, hp, i: (bi, 0, 2 * lane_blocks + hp)),
                  _resident((1, dh)), _resident((1, dh)), _resident((tq, tq))],
        out_specs=pl.BlockSpec((1, tq, 128), lambda bi, hp, i: (bi, i, hp)),
        out_shape=jax.ShapeDtypeStruct((b, s, d), BF16),
        scratch_shapes=[pltpu.VMEM((128 // dh, s, dh), BF16), pltpu.VMEM((128 // dh, s, dh), BF16)],
        compiler_params=_cparams(3),
        name="sb_attn",
    )(proj, proj, proj, qg, kg, u)


def _rope_tables(seq, dk):
    half = dk // 2
    inv_freq = ROPE_BASE ** (-jnp.arange(half, dtype=F32) / half)
    ang = jnp.arange(seq, dtype=F32)[:, None] * inv_freq[None, :]
    return jnp.cos(ang), jnp.sin(ang)


def kernel(x, norm_mix, norm_ffn, ret_w_in, ret_q_norm, ret_k_norm, ret_gn_w, ret_gn_b, ret_w_out,
           conv_pw1_w, conv_pw1_b, conv_dw_w, conv_dw_b, conv_ln_w, conv_ln_b, conv_pw2_w, conv_pw2_b,
           sb_w_in, sb_q_norm, sb_k_norm, sb_w_out, ffn_w1, ffn_w2, final_norm):
    b, s, d = x.shape
    depth = norm_mix.shape[0]
    t = b * s
    tm = min(512, s)
    cos, sin = _rope_tables(s, d // RET_HEADS)
    zero_bias = jnp.zeros((1, d), F32)
    row = lambda a: a.reshape(1, -1)

    x2 = x.reshape(t, d)
    for i in range(depth):
        kind = i % N_MIXERS
        j = i // N_MIXERS
        nw = row(norm_mix[i])
        if kind == 0:
            q, k, v, g = ret_proj(x2, nw, ret_w_in[j].astype(BF16), row(ret_q_norm[j]),
                                  row(ret_k_norm[j]), cos, sin, tm=tm, seq=s)
            z = ret_core(q.reshape(b, s, d), k.reshape(b, s, d), v.reshape(b, s, 2 * d),
                         g.reshape(b, s, 2 * d), row(ret_gn_w[j]), row(ret_gn_b[j]),
                         blk=min(256, s))
            x2 = mm_res(z.reshape(t, 2 * d), ret_w_out[j].astype(BF16), zero_bias, x2, tm=tm)
        elif kind == 1:
            u = norm_glu(x2, nw, conv_pw1_w[j].astype(BF16), row(conv_pw1_b[j]), tm=tm)
            x2 = conv_tail(u.reshape(b, s, d), conv_dw_w[j], row(conv_dw_b[j]), row(conv_ln_w[j]),
                           row(conv_ln_b[j]), conv_pw2_w[j].astype(BF16), row(conv_pw2_b[j]),
                           x2.reshape(b, s, d), tc=min(128, s)).reshape(t, d)
        else:
            proj = norm_proj(x2, nw, sb_w_in[j].astype(BF16), tm=tm)
            y = sb_attention(proj.reshape(b, s, 3 * d), row(sb_q_norm[j]), row(sb_k_norm[j]),
                             tq=min(256, s))
            x2 = mm_res(y.reshape(t, d), sb_w_out[j].astype(BF16), zero_bias, x2, tm=tm)
        x2 = ffn(x2, row(norm_ffn[i]), ffn_w1[i].astype(BF16), ffn_w2[i].astype(BF16),
                 row(final_norm), tm=tm, final_norm=(i == depth - 1))
    return x2.reshape(b, s, d)
```

```python
import functools

import jax
import jax.numpy as jnp
from jax import lax
from jax.experimental import pallas as pl
from jax.experimental.pallas import tpu as pltpu

F32 = jnp.float32
BF16 = jnp.bfloat16
EPS = 1e-6

CHUNK = 64
RET_HEADS = 4
ROPE_BASE = 10000.0
CONV_WIDTH = 31
SB_HEADS = 16
N_MIXERS = 3

V7X_VMEM_LIMIT_BYTES = 56 * 1024 * 1024
SUBLANES = 8
CONV_HALO = 32


def _cparams(n_axes):
    return pltpu.CompilerParams(
        dimension_semantics=("arbitrary",) * n_axes,
        vmem_limit_bytes=V7X_VMEM_LIMIT_BYTES)


def _resident(shape):
    zeros = (0,) * len(shape)
    return pl.BlockSpec(shape, lambda *_: zeros, pipeline_mode=pl.Buffered(1))


def _rms(x, g):
    return x * lax.rsqrt(jnp.mean(x * x, axis=-1, keepdims=True) + EPS) * g


def _dot(a, b):
    return jnp.dot(a, b, preferred_element_type=F32)


def _dot_nt(a, b):
    return lax.dot_general(a, b, (((1,), (1,)), ((), ())), preferred_element_type=F32)


def _dot_tn(a, b):
    return lax.dot_general(a, b, (((0,), (0,)), ((), ())), preferred_element_type=F32)


def _norm_proj_kernel(x_ref, nw_ref, w_ref, o_ref, *, chunk):
    xn = _rms(x_ref[...], nw_ref[...]).astype(BF16)
    for c in range(0, o_ref.shape[1], chunk):
        o_ref[:, c:c + chunk] = _dot(xn, w_ref[:, c:c + chunk]).astype(o_ref.dtype)


def norm_proj(x, nw, w, *, tm, out_dtype=F32, chunk=512):
    t, d = x.shape
    n = w.shape[1]
    return pl.pallas_call(
        functools.partial(_norm_proj_kernel, chunk=chunk),
        grid=(t // tm,),
        in_specs=[pl.BlockSpec((tm, d), lambda i: (i, 0)),
                  _resident((1, d)), _resident((d, n))],
        out_specs=pl.BlockSpec((tm, n), lambda i: (i, 0)),
        out_shape=jax.ShapeDtypeStruct((t, n), out_dtype),
        compiler_params=_cparams(1),
        name="norm_proj",
    )(x, nw, w)


def _norm_glu_kernel(x_ref, nw_ref, w_ref, b_ref, o_ref, *, chunk):
    d = o_ref.shape[1]
    xn = _rms(x_ref[...], nw_ref[...]).astype(BF16)
    for c in range(0, d, chunk):
        a = _dot(xn, w_ref[:, c:c + chunk]) + b_ref[:, c:c + chunk]
        gate = _dot(xn, w_ref[:, d + c:d + c + chunk]) + b_ref[:, d + c:d + c + chunk]
        o_ref[:, c:c + chunk] = a * jax.nn.sigmoid(gate)


def norm_glu(x, nw, w, b, *, tm, chunk=512):
    t, d = x.shape
    return pl.pallas_call(
        functools.partial(_norm_glu_kernel, chunk=chunk),
        grid=(t // tm,),
        in_specs=[pl.BlockSpec((tm, d), lambda i: (i, 0)),
                  _resident((1, d)), _resident((d, 2 * d)), _resident((1, 2 * d))],
        out_specs=pl.BlockSpec((tm, d), lambda i: (i, 0)),
        out_shape=jax.ShapeDtypeStruct((t, d), F32),
        compiler_params=_cparams(1),
        name="norm_glu",
    )(x, nw, w, b)


def _mm_res_kernel(z_ref, w_ref, b_ref, x_ref, o_ref):
    o_ref[...] = x_ref[...] + _dot(z_ref[...], w_ref[...]) + b_ref[...]


def mm_res(z, w, b, x, *, tm):
    t, kdim = z.shape
    d = w.shape[1]
    return pl.pallas_call(
        _mm_res_kernel,
        grid=(t // tm,),
        in_specs=[pl.BlockSpec((tm, kdim), lambda i: (i, 0)),
                  _resident((kdim, d)), _resident((1, d)),
                  pl.BlockSpec((tm, d), lambda i: (i, 0))],
        out_specs=pl.BlockSpec((tm, d), lambda i: (i, 0)),
        out_shape=jax.ShapeDtypeStruct((t, d), F32),
        compiler_params=_cparams(1),
        name="mm_res",
    )(z, w, b, x)


def _ffn_kernel(x_ref, nw_ref, w1_ref, w2_ref, fw_ref, o_ref, h_scr, *, chunk, final_norm):
    x = x_ref[...]
    xn = _rms(x, nw_ref[...]).astype(BF16)
    for c in range(0, h_scr.shape[1], chunk):
        h = jnp.maximum(_dot(xn, w1_ref[:, c:c + chunk]), 0.0)
        h_scr[:, c:c + chunk] = (h * h).astype(BF16)
    y = x + _dot(h_scr[...], w2_ref[...])
    if final_norm:
        y = _rms(y, fw_ref[...])
    o_ref[...] = y


def ffn(x, nw, w1, w2, fw, *, tm, final_norm, chunk=1024):
    t, d = x.shape
    dff = w1.shape[1]
    return pl.pallas_call(
        functools.partial(_ffn_kernel, chunk=chunk, final_norm=final_norm),
        grid=(t // tm,),
        in_specs=[pl.BlockSpec((tm, d), lambda i: (i, 0)),
                  _resident((1, d)), _resident((d, dff)), _resident((dff, d)),
                  _resident((1, d))],
        out_specs=pl.BlockSpec((tm, d), lambda i: (i, 0)),
        out_shape=jax.ShapeDtypeStruct((t, d), F32),
        scratch_shapes=[pltpu.VMEM((tm, dff), BF16)],
        compiler_params=_cparams(1),
        name="ffn",
    )(x, nw, w1, w2, fw)


def _ret_proj_kernel(x_ref, nw_ref, w_ref, qg_ref, kg_ref, cos_ref, sin_ref,
                     q_ref, k_ref, v_ref, g_ref, *, heads, chunk):
    d = q_ref.shape[1]
    dk = d // heads
    half = dk // 2
    xn = _rms(x_ref[...], nw_ref[...]).astype(BF16)
    cos = cos_ref[...]
    sin = sin_ref[...]
    for base, gain_ref, o_ref, scale in ((0, qg_ref, q_ref, 1.0), (d, kg_ref, k_ref, dk ** -0.5)):
        for h in range(heads):
            y = _rms(_dot(xn, w_ref[:, base + h * dk:base + (h + 1) * dk]), gain_ref[...])
            t1 = y[:, :half]
            t2 = y[:, half:]
            o_ref[:, h * dk:h * dk + half] = ((t1 * cos - t2 * sin) * scale).astype(BF16)
            o_ref[:, h * dk + half:(h + 1) * dk] = ((t1 * sin + t2 * cos) * scale).astype(BF16)
    for c in range(0, 2 * d, chunk):
        v_ref[:, c:c + chunk] = _dot(xn, w_ref[:, 2 * d + c:2 * d + c + chunk]).astype(BF16)
        g_ref[:, c:c + chunk] = _dot(xn, w_ref[:, 4 * d + c:4 * d + c + chunk])


def ret_proj(x, nw, w, qg, kg, cos, sin, *, tm, seq, chunk=512):
    t, d = x.shape
    spb = seq // tm
    row = lambda i: (i, 0)
    return pl.pallas_call(
        functools.partial(_ret_proj_kernel, heads=RET_HEADS, chunk=chunk),
        grid=(t // tm,),
        in_specs=[pl.BlockSpec((tm, d), row),
                  _resident((1, d)), _resident((d, 6 * d)),
                  _resident(qg.shape), _resident(kg.shape),
                  pl.BlockSpec((tm, cos.shape[1]), lambda i: (i % spb, 0)),
                  pl.BlockSpec((tm, sin.shape[1]), lambda i: (i % spb, 0))],
        out_specs=[pl.BlockSpec((tm, d), row), pl.BlockSpec((tm, d), row),
                   pl.BlockSpec((tm, 2 * d), row), pl.BlockSpec((tm, 2 * d), row)],
        out_shape=[jax.ShapeDtypeStruct((t, d), BF16), jax.ShapeDtypeStruct((t, d), BF16),
                   jax.ShapeDtypeStruct((t, 2 * d), BF16), jax.ShapeDtypeStruct((t, 2 * d), F32)],
        compiler_params=_cparams(1),
        name="ret_proj",
    )(x, nw, w, qg, kg, cos, sin)


def _ret_core_kernel(q_ref, k_ref, v_ref, g_ref, dm_ref, qd_ref, kd_ref, cd_ref,
                     gw_ref, gb_ref, z_ref, state):
    @pl.when(pl.program_id(1) == 0)
    def _():
        state[...] = jnp.zeros_like(state)

    heads, dk, dv = state.shape
    for h in range(heads):
        q = q_ref[0, :, h * dk:(h + 1) * dk]
        k = k_ref[0, :, h * dk:(h + 1) * dk]
        v = v_ref[0, :, h * dv:(h + 1) * dv]
        st = state[h]
        p = (_dot_nt(q, k) * dm_ref[h]).astype(BF16)
        y = _dot(p, v) + qd_ref[h] * _dot(q, st.astype(BF16))
        kk = (k.astype(F32) * kd_ref[h]).astype(BF16)
        state[h] = st * cd_ref[h] + _dot_tn(kk, v)

        mu = jnp.mean(y, axis=-1, keepdims=True)
        yc = y - mu
        var = jnp.mean(yc * yc, axis=-1, keepdims=True)
        yn = (yc * lax.rsqrt(var + EPS) * gw_ref[:, h * dv:(h + 1) * dv]
              + gb_ref[:, h * dv:(h + 1) * dv])
        g = g_ref[0, :, h * dv:(h + 1) * dv]
        z_ref[0, :, h * dv:(h + 1) * dv] = (g * jax.nn.sigmoid(g) * yn).astype(BF16)


def _ret_decay_tables(blk):
    h = jnp.arange(RET_HEADS, dtype=F32)
    log_g = jnp.log(1.0 - jnp.exp2(-5.0 - h))
    idx = jnp.arange(blk, dtype=F32)
    dist = idx[:, None] - idx[None, :]
    ct = jnp.arange(blk)[:, None] // CHUNK
    cs = jnp.arange(blk)[None, :] // CHUNK
    expo = jnp.where(ct == cs, jnp.abs(dist), dist)
    dm = jnp.where((cs <= ct)[None], jnp.exp(log_g[:, None, None] * expo[None]), 0.0)
    qd = jnp.exp(log_g[:, None] * (idx + 1.0))[..., None]
    kd = jnp.exp(log_g[:, None] * (blk - 1.0 - idx))[..., None]
    cd = jnp.exp(log_g * blk)[:, None, None]
    return dm, qd, kd, cd


def ret_core(q, k, v, g, gn_w, gn_b, *, blk):
    b, s, d = q.shape
    dk = d // RET_HEADS
    dv = 2 * d // RET_HEADS
    dm, qd, kd, cd = _ret_decay_tables(blk)
    tok = lambda bi, n: (bi, n, 0)
    return pl.pallas_call(
        _ret_core_kernel,
        grid=(b, s // blk),
        in_specs=[pl.BlockSpec((1, blk, d), tok), pl.BlockSpec((1, blk, d), tok),
                  pl.BlockSpec((1, blk, 2 * d), tok), pl.BlockSpec((1, blk, 2 * d), tok),
                  _resident(dm.shape), _resident(qd.shape), _resident(kd.shape), _resident(cd.shape),
                  _resident((1, 2 * d)), _resident((1, 2 * d))],
        out_specs=pl.BlockSpec((1, blk, 2 * d), tok),
        out_shape=jax.ShapeDtypeStruct((b, s, 2 * d), BF16),
        scratch_shapes=[pltpu.VMEM((RET_HEADS, dk, dv), F32)],
        compiler_params=_cparams(2),
        name="ret_core",
    )(q, k, v, g, dm, qd, kd, cd, gn_w, gn_b)


def _conv_tail_kernel(u_ref, halo_ref, dw_ref, dwb_ref, lw_ref, lb_ref, w2_ref, b2_ref, x_ref,
                      o_ref, pad_scr, conv_scr, taps_scr, *, rows):
    tc = u_ref.shape[1]
    width = dw_ref.shape[0]
    halo = halo_ref[0]
    pad_scr[0, 0:CONV_HALO, :] = jnp.where(pl.program_id(1) > 0, halo, jnp.zeros_like(halo))
    pad_scr[0, CONV_HALO:CONV_HALO + tc, :] = u_ref[0]
    span = pad_scr.shape[1] - SUBLANES
    for r in range(1, SUBLANES):
        pad_scr[r, 0:span, :] = pad_scr[0, r:r + span, :]

    @pl.when((pl.program_id(0) == 0) & (pl.program_id(1) == 0))
    def _():
        for j in range(width):
            taps_scr[j] = jnp.broadcast_to(dw_ref[j:j + 1, :], taps_scr.shape[1:])

    first = CONV_HALO - (width - 1)
    d = conv_scr.shape[1]
    for r0 in range(0, tc, rows):
        acc = None
        for j in range(width):
            shift, base = (first + j) % SUBLANES, (first + j) // SUBLANES * SUBLANES
            window = pad_scr[shift, r0 + base:r0 + base + rows, :]
            term = taps_scr[j][None] * window.reshape(rows // SUBLANES, SUBLANES, d)
            acc = term if acc is None else acc + term
        conv_scr[r0:r0 + rows, :] = acc.reshape(rows, d)
    hc = conv_scr[...] + dwb_ref[...]
    mu = jnp.mean(hc, axis=-1, keepdims=True)
    c = hc - mu
    var = jnp.mean(c * c, axis=-1, keepdims=True)
    hn = c * lax.rsqrt(var + EPS) * lw_ref[...] + lb_ref[...]
    act = (hn * jax.nn.sigmoid(hn)).astype(BF16)
    o_ref[0] = x_ref[0] + _dot(act, w2_ref[...]) + b2_ref[...]


def conv_tail(u, dw_w, dw_b, ln_w, ln_b, w2, b2, x, *, tc, rows=16):
    b, s, d = u.shape
    per_tile = tc // CONV_HALO
    tok = lambda bi, n: (bi, n, 0)
    return pl.pallas_call(
        functools.partial(_conv_tail_kernel, rows=rows),
        grid=(b, s // tc),
        in_specs=[pl.BlockSpec((1, tc, d), tok),
                  pl.BlockSpec((1, CONV_HALO, d),
                               lambda bi, n: (bi, jnp.maximum(n * per_tile - 1, 0), 0)),
                  _resident(dw_w.shape), _resident((1, d)), _resident((1, d)), _resident((1, d)),
                  _resident((d, d)), _resident((1, d)),
                  pl.BlockSpec((1, tc, d), tok)],
        out_specs=pl.BlockSpec((1, tc, d), tok),
        out_shape=jax.ShapeDtypeStruct((b, s, d), F32),
        scratch_shapes=[pltpu.VMEM((SUBLANES, CONV_HALO + tc, d), F32), pltpu.VMEM((tc, d), F32),
                        pltpu.VMEM((dw_w.shape[0], SUBLANES, d), F32)],
        compiler_params=_cparams(2),
        name="conv_tail",
    )(u, u, dw_w, dw_b, ln_w, ln_b, w2, b2, x)


LOG2E = 1.4426950408889634


def _sb_block(q, kb, u2, carry, causal):
    tk = kb.shape[0]

    def mask_own_rows(t):
        own = jnp.where(causal, t[:tk], 0.0)
        return own if t.shape[0] == tk else jnp.concatenate([own, t[tk:]], axis=0)

    z = _dot_nt(q, kb)
    sp = jnp.maximum(z, 0.0) + jnp.log(1.0 + jnp.exp2(-jnp.abs(z))) * LOG2E
    if causal is not None:
        sp = mask_own_rows(sp)
    r = _dot(sp.astype(BF16), u2)
    a = jnp.exp2(z - r - jnp.concatenate([carry] * (tk // carry.shape[1]), axis=1))
    if causal is not None:
        a = mask_own_rows(a)
    return a.astype(BF16), jnp.broadcast_to(r[:, 0:1], carry.shape)


def _sb_attn_kernel(q_ref, k_ref, v_ref, qg_ref, kg_ref, u2_ref, o_ref,
                    kn_scr, vn_scr, qn_scr, acc_scr, car_scr, *, dh, tk, mrows):
    i = pl.program_id(2)
    tq = q_ref.shape[1]
    nsub = tq // tk
    pair = 128 // dh

    def head_lanes(t, e):
        lane = lax.broadcasted_iota(jnp.int32, t.shape, 1)
        return jnp.where(lane // dh == e, t, 0.0)

    def head_rms(t, e, gain):
        te = head_lanes(t, e)
        ms = jnp.sum(te * te, axis=-1, keepdims=True) * (1.0 / dh)
        return te * lax.rsqrt(ms + EPS) * gain

    @pl.when(i == 0)
    def _():
        for e in range(pair):
            kn_scr[e] = head_rms(k_ref[0], e, kg_ref[...]).astype(BF16)
            vn_scr[e] = head_lanes(v_ref[0], e).astype(BF16)

    for e in range(pair):
        qn_scr[e] = (head_rms(q_ref[0], e, qg_ref[...]) * (LOG2E * dh ** -0.5)).astype(BF16)
    acc_scr[...] = jnp.zeros_like(acc_scr)
    car_scr[...] = jnp.zeros_like(car_scr)

    u2 = u2_ref[...]
    row = lax.broadcasted_iota(jnp.int32, (tk, tk), 0)
    col = lax.broadcasted_iota(jnp.int32, (tk, tk), 1)
    causal = col < row

    def step(start, lo_row, mask):
        vv = jnp.concatenate([vn_scr[e, pl.ds(start, tk), :] for e in range(pair)], axis=0)
        for r0 in range(lo_row, tq, mrows):
            r1 = min(r0 + mrows, tq)
            weights = []
            for e in range(pair):
                a, total = _sb_block(qn_scr[e, r0:r1, :], kn_scr[e, pl.ds(start, tk), :], u2,
                                     car_scr[e, r0:r1, :], mask if r0 == lo_row else None)
                car_scr[e, r0:r1, :] += total
                weights.append(a)
            acc_scr[r0:r1, :] += _dot(jnp.concatenate(weights, axis=1), vv)

    for c in reversed(range(nsub)):
        step(pl.multiple_of(i * tq + c * tk, tk), c * tk, causal)

    def body(st, _):
        step(pl.multiple_of((i * nsub - 1 - st) * tk, tk), 0, None)
        return 0

    lax.fori_loop(0, i * nsub, body, 0)
    o_ref[0] = acc_scr[...].astype(o_ref.dtype)


def sb_attention(proj, qg, kg, *, tq, tk, mrows):
    b, s, d3 = proj.shape
    d = d3 // 3
    dh = d // SB_HEADS
    pair = 128 // dh
    lane_blocks = d // 128
    idx = jnp.arange(tk)
    u2 = (idx[:, None] >= idx[None, :]).astype(BF16)
    return pl.pallas_call(
        functools.partial(_sb_attn_kernel, dh=dh, tk=tk, mrows=mrows),
        grid=(b, lane_blocks, s // tq),
        in_specs=[pl.BlockSpec((1, tq, 128), lambda bi, hp, i: (bi, i, hp)),
                  pl.BlockSpec((1, s, 128), lambda bi, hp, i: (bi, 0, lane_blocks + hp)),
                  pl.BlockSpec((1, s, 128), lambda bi, hp, i: (bi, 0, 2 * lane_blocks + hp)),
                  _resident((1, 128)), _resident((1, 128)), _resident((tk, tk))],
        out_specs=pl.BlockSpec((1, tq, 128), lambda bi, hp, i: (bi, i, hp)),
        out_shape=jax.ShapeDtypeStruct((b, s, d), BF16),
        scratch_shapes=[pltpu.VMEM((pair, s, 128), BF16), pltpu.VMEM((pair, s, 128), BF16),
                        pltpu.VMEM((pair, tq, 128), BF16), pltpu.VMEM((tq, 128), F32),
                        pltpu.VMEM((pair, tq, 128), F32)],
        compiler_params=_cparams(3),
        name="sb_attn",
    )(proj, proj, proj, jnp.tile(qg, (1, pair)), jnp.tile(kg, (1, pair)), u2)


def _rope_tables(seq, dk):
    half = dk // 2
    inv_freq = ROPE_BASE ** (-jnp.arange(half, dtype=F32) / half)
    ang = jnp.arange(seq, dtype=F32)[:, None] * inv_freq[None, :]
    return jnp.cos(ang), jnp.sin(ang)


def kernel(x, norm_mix, norm_ffn, ret_w_in, ret_q_norm, ret_k_norm, ret_gn_w, ret_gn_b, ret_w_out,
           conv_pw1_w, conv_pw1_b, conv_dw_w, conv_dw_b, conv_ln_w, conv_ln_b, conv_pw2_w, conv_pw2_b,
           sb_w_in, sb_q_norm, sb_k_norm, sb_w_out, ffn_w1, ffn_w2, final_norm):
    b, s, d = x.shape
    depth = norm_mix.shape[0]
    t = b * s
    tm = min(512, s)
    cos, sin = _rope_tables(s, d // RET_HEADS)
    zero_bias = jnp.zeros((1, d), F32)
    row = lambda a: a.reshape(1, -1)

    x2 = x.reshape(t, d)
    for i in range(depth):
        kind = i % N_MIXERS
        j = i // N_MIXERS
        nw = row(norm_mix[i])
        if kind == 0:
            q, k, v, g = ret_proj(x2, nw, ret_w_in[j].astype(BF16), row(ret_q_norm[j]),
                                  row(ret_k_norm[j]), cos, sin, tm=tm, seq=s)
            z = ret_core(q.reshape(b, s, d), k.reshape(b, s, d), v.reshape(b, s, 2 * d),
                         g.reshape(b, s, 2 * d), row(ret_gn_w[j]), row(ret_gn_b[j]),
                         blk=min(256, s))
            x2 = mm_res(z.reshape(t, 2 * d), ret_w_out[j].astype(BF16), zero_bias, x2, tm=tm)
        elif kind == 1:
            u = norm_glu(x2, nw, conv_pw1_w[j].astype(BF16), row(conv_pw1_b[j]), tm=tm)
            x2 = conv_tail(u.reshape(b, s, d), conv_dw_w[j], row(conv_dw_b[j]), row(conv_ln_w[j]),
                           row(conv_ln_b[j]), conv_pw2_w[j].astype(BF16), row(conv_pw2_b[j]),
                           x2.reshape(b, s, d), tc=min(256, s)).reshape(t, d)
        else:
            proj = norm_proj(x2, nw, sb_w_in[j].astype(BF16), tm=tm)
            y = sb_attention(proj.reshape(b, s, 3 * d), row(sb_q_norm[j]), row(sb_k_norm[j]),
                             tq=min(1024, s), tk=min(256, s), mrows=min(512, s))
            x2 = mm_res(y.reshape(t, d), sb_w_out[j].astype(BF16), zero_bias, x2, tm=tm)
        x2 = ffn(x2, row(norm_ffn[i]), ffn_w1[i].astype(BF16), ffn_w2[i].astype(BF16),
                 row(final_norm), tm=tm, final_norm=(i == depth - 1))
    return x2.reshape(b, s, d)
```

```python
import functools

import jax
import jax.numpy as jnp
from jax import lax
from jax.experimental import pallas as pl
from jax.experimental.pallas import tpu as pltpu

F32 = jnp.float32
BF16 = jnp.bfloat16
EPS = 1e-6

CHUNK = 64
RET_HEADS = 4
ROPE_BASE = 10000.0
CONV_WIDTH = 31
SB_HEADS = 16
N_MIXERS = 3

V7X_VMEM_LIMIT_BYTES = 56 * 1024 * 1024
SUBLANES = 8
CONV_HALO = 32


def _cparams(n_axes):
    return pltpu.CompilerParams(
        dimension_semantics=("arbitrary",) * n_axes,
        vmem_limit_bytes=V7X_VMEM_LIMIT_BYTES)


def _resident(shape):
    zeros = (0,) * len(shape)
    return pl.BlockSpec(shape, lambda *_: zeros, pipeline_mode=pl.Buffered(1))


def _layer(stack, layer):
    return pl.BlockSpec((None,) + stack.shape[1:], lambda *_: (layer, 0, 0),
                        pipeline_mode=pl.Buffered(1))


def _rms(x, g):
    return x * lax.rsqrt(jnp.mean(x * x, axis=-1, keepdims=True) + EPS) * g


def _dot(a, b):
    return jnp.dot(a, b, preferred_element_type=F32)


def _dot_nt(a, b):
    return lax.dot_general(a, b, (((1,), (1,)), ((), ())), preferred_element_type=F32)


def _dot_tn(a, b):
    return lax.dot_general(a, b, (((0,), (0,)), ((), ())), preferred_element_type=F32)


def _norm_proj_kernel(x_ref, nw_ref, w_ref, o_ref, *, chunk):
    xn = _rms(x_ref[...], nw_ref[...]).astype(BF16)
    for c in range(0, o_ref.shape[1], chunk):
        o_ref[:, c:c + chunk] = _dot(xn, w_ref[:, c:c + chunk]).astype(o_ref.dtype)


def norm_proj(x, nw, w, *, layer, tm, out_dtype=F32, chunk=512):
    t, d = x.shape
    n = w.shape[2]
    return pl.pallas_call(
        functools.partial(_norm_proj_kernel, chunk=chunk),
        grid=(t // tm,),
        in_specs=[pl.BlockSpec((tm, d), lambda i: (i, 0)),
                  _resident((1, d)), _layer(w, layer)],
        out_specs=pl.BlockSpec((tm, n), lambda i: (i, 0)),
        out_shape=jax.ShapeDtypeStruct((t, n), out_dtype),
        compiler_params=_cparams(1),
        name="norm_proj",
    )(x, nw, w)


def _norm_glu_kernel(x_ref, nw_ref, w_ref, b_ref, o_ref, *, chunk):
    d = o_ref.shape[1]
    xn = _rms(x_ref[...], nw_ref[...]).astype(BF16)
    for c in range(0, d, chunk):
        a = _dot(xn, w_ref[:, c:c + chunk]) + b_ref[:, c:c + chunk]
        gate = _dot(xn, w_ref[:, d + c:d + c + chunk]) + b_ref[:, d + c:d + c + chunk]
        o_ref[:, c:c + chunk] = a * jax.nn.sigmoid(gate)


def norm_glu(x, nw, w, b, *, layer, tm, chunk=512):
    t, d = x.shape
    return pl.pallas_call(
        functools.partial(_norm_glu_kernel, chunk=chunk),
        grid=(t // tm,),
        in_specs=[pl.BlockSpec((tm, d), lambda i: (i, 0)),
                  _resident((1, d)), _layer(w, layer), _resident((1, 2 * d))],
        out_specs=pl.BlockSpec((tm, d), lambda i: (i, 0)),
        out_shape=jax.ShapeDtypeStruct((t, d), F32),
        compiler_params=_cparams(1),
        name="norm_glu",
    )(x, nw, w, b)


def _mm_res_kernel(z_ref, w_ref, b_ref, x_ref, o_ref):
    o_ref[...] = x_ref[...] + _dot(z_ref[...], w_ref[...]) + b_ref[...]


def mm_res(z, w, b, x, *, layer, tm):
    t, kdim = z.shape
    d = w.shape[2]
    return pl.pallas_call(
        _mm_res_kernel,
        grid=(t // tm,),
        in_specs=[pl.BlockSpec((tm, kdim), lambda i: (i, 0)),
                  _layer(w, layer), _resident((1, d)),
                  pl.BlockSpec((tm, d), lambda i: (i, 0))],
        out_specs=pl.BlockSpec((tm, d), lambda i: (i, 0)),
        out_shape=jax.ShapeDtypeStruct((t, d), F32),
        compiler_params=_cparams(1),
        name="mm_res",
    )(z, w, b, x)


def _ffn_kernel(x_ref, nw_ref, w1_ref, w2_ref, fw_ref, o_ref, h_scr, *, chunk, final_norm):
    x = x_ref[...]
    xn = _rms(x, nw_ref[...]).astype(BF16)
    for c in range(0, h_scr.shape[1], chunk):
        h = jnp.maximum(_dot(xn, w1_ref[:, c:c + chunk]), 0.0)
        h_scr[:, c:c + chunk] = (h * h).astype(BF16)
    y = x + _dot(h_scr[...], w2_ref[...])
    if final_norm:
        y = _rms(y, fw_ref[...])
    o_ref[...] = y


def ffn(x, nw, w1, w2, fw, *, layer, tm, final_norm, chunk=1024):
    t, d = x.shape
    dff = w1.shape[2]
    return pl.pallas_call(
        functools.partial(_ffn_kernel, chunk=chunk, final_norm=final_norm),
        grid=(t // tm,),
        in_specs=[pl.BlockSpec((tm, d), lambda i: (i, 0)),
                  _resident((1, d)), _layer(w1, layer), _layer(w2, layer),
                  _resident((1, d))],
        out_specs=pl.BlockSpec((tm, d), lambda i: (i, 0)),
        out_shape=jax.ShapeDtypeStruct((t, d), F32),
        scratch_shapes=[pltpu.VMEM((tm, dff), BF16)],
        compiler_params=_cparams(1),
        name="ffn",
    )(x, nw, w1, w2, fw)


def _ret_proj_kernel(x_ref, nw_ref, w_ref, qg_ref, kg_ref, cos_ref, sin_ref,
                     q_ref, k_ref, v_ref, g_ref, *, heads, chunk):
    d = q_ref.shape[1]
    dk = d // heads
    half = dk // 2
    xn = _rms(x_ref[...], nw_ref[...]).astype(BF16)
    cos = cos_ref[...]
    sin = sin_ref[...]
    for base, gain_ref, o_ref, scale in ((0, qg_ref, q_ref, 1.0), (d, kg_ref, k_ref, dk ** -0.5)):
        for h in range(heads):
            y = _rms(_dot(xn, w_ref[:, base + h * dk:base + (h + 1) * dk]), gain_ref[...])
            t1 = y[:, :half]
            t2 = y[:, half:]
            o_ref[:, h * dk:h * dk + half] = ((t1 * cos - t2 * sin) * scale).astype(BF16)
            o_ref[:, h * dk + half:(h + 1) * dk] = ((t1 * sin + t2 * cos) * scale).astype(BF16)
    for c in range(0, 2 * d, chunk):
        v_ref[:, c:c + chunk] = _dot(xn, w_ref[:, 2 * d + c:2 * d + c + chunk]).astype(BF16)
        g_ref[:, c:c + chunk] = _dot(xn, w_ref[:, 4 * d + c:4 * d + c + chunk])


def ret_proj(x, nw, w, qg, kg, cos, sin, *, layer, tm, seq, chunk=512):
    t, d = x.shape
    spb = seq // tm
    row = lambda i: (i, 0)
    return pl.pallas_call(
        functools.partial(_ret_proj_kernel, heads=RET_HEADS, chunk=chunk),
        grid=(t // tm,),
        in_specs=[pl.BlockSpec((tm, d), row),
                  _resident((1, d)), _layer(w, layer),
                  _resident(qg.shape), _resident(kg.shape),
                  pl.BlockSpec((tm, cos.shape[1]), lambda i: (i % spb, 0)),
                  pl.BlockSpec((tm, sin.shape[1]), lambda i: (i % spb, 0))],
        out_specs=[pl.BlockSpec((tm, d), row), pl.BlockSpec((tm, d), row),
                   pl.BlockSpec((tm, 2 * d), row), pl.BlockSpec((tm, 2 * d), row)],
        out_shape=[jax.ShapeDtypeStruct((t, d), BF16), jax.ShapeDtypeStruct((t, d), BF16),
                   jax.ShapeDtypeStruct((t, 2 * d), BF16), jax.ShapeDtypeStruct((t, 2 * d), F32)],
        compiler_params=_cparams(1),
        name="ret_proj",
    )(x, nw, w, qg, kg, cos, sin)


def _ret_core_kernel(q_ref, k_ref, v_ref, g_ref, dm_ref, qd_ref, kd_ref, cd_ref,
                     gw_ref, gb_ref, z_ref, state):
    @pl.when(pl.program_id(1) == 0)
    def _():
        state[...] = jnp.zeros_like(state)

    heads, dk, dv = state.shape
    for h in range(heads):
        q = q_ref[0, :, h * dk:(h + 1) * dk]
        k = k_ref[0, :, h * dk:(h + 1) * dk]
        v = v_ref[0, :, h * dv:(h + 1) * dv]
        st = state[h]
        p = (_dot_nt(q, k) * dm_ref[h]).astype(BF16)
        y = _dot(p, v) + qd_ref[h] * _dot(q, st.astype(BF16))
        kk = (k.astype(F32) * kd_ref[h]).astype(BF16)
        state[h] = st * cd_ref[h] + _dot_tn(kk, v)

        mu = jnp.mean(y, axis=-1, keepdims=True)
        yc = y - mu
        var = jnp.mean(yc * yc, axis=-1, keepdims=True)
        yn = (yc * lax.rsqrt(var + EPS) * gw_ref[:, h * dv:(h + 1) * dv]
              + gb_ref[:, h * dv:(h + 1) * dv])
        g = g_ref[0, :, h * dv:(h + 1) * dv]
        z_ref[0, :, h * dv:(h + 1) * dv] = (g * jax.nn.sigmoid(g) * yn).astype(BF16)


def _ret_decay_tables(blk):
    h = jnp.arange(RET_HEADS, dtype=F32)
    log_g = jnp.log(1.0 - jnp.exp2(-5.0 - h))
    idx = jnp.arange(blk, dtype=F32)
    dist = idx[:, None] - idx[None, :]
    ct = jnp.arange(blk)[:, None] // CHUNK
    cs = jnp.arange(blk)[None, :] // CHUNK
    expo = jnp.where(ct == cs, jnp.abs(dist), dist)
    dm = jnp.where((cs <= ct)[None], jnp.exp(log_g[:, None, None] * expo[None]), 0.0)
    qd = jnp.exp(log_g[:, None] * (idx + 1.0))[..., None]
    kd = jnp.exp(log_g[:, None] * (blk - 1.0 - idx))[..., None]
    cd = jnp.exp(log_g * blk)[:, None, None]
    return dm, qd, kd, cd


def ret_core(q, k, v, g, gn_w, gn_b, *, blk):
    b, s, d = q.shape
    dk = d // RET_HEADS
    dv = 2 * d // RET_HEADS
    dm, qd, kd, cd = _ret_decay_tables(blk)
    tok = lambda bi, n: (bi, n, 0)
    return pl.pallas_call(
        _ret_core_kernel,
        grid=(b, s // blk),
        in_specs=[pl.BlockSpec((1, blk, d), tok), pl.BlockSpec((1, blk, d), tok),
                  pl.BlockSpec((1, blk, 2 * d), tok), pl.BlockSpec((1, blk, 2 * d), tok),
                  _resident(dm.shape), _resident(qd.shape), _resident(kd.shape), _resident(cd.shape),
                  _resident((1, 2 * d)), _resident((1, 2 * d))],
        out_specs=pl.BlockSpec((1, blk, 2 * d), tok),
        out_shape=jax.ShapeDtypeStruct((b, s, 2 * d), BF16),
        scratch_shapes=[pltpu.VMEM((RET_HEADS, dk, dv), F32)],
        compiler_params=_cparams(2),
        name="ret_core",
    )(q, k, v, g, dm, qd, kd, cd, gn_w, gn_b)


def _conv_tail_kernel(u_ref, halo_ref, dw_ref, dwb_ref, lw_ref, lb_ref, w2_ref, b2_ref, x_ref,
                      o_ref, pad_scr, conv_scr, taps_scr, *, rows):
    tc = u_ref.shape[1]
    width = dw_ref.shape[0]
    halo = halo_ref[0]
    pad_scr[0, 0:CONV_HALO, :] = jnp.where(pl.program_id(1) > 0, halo, jnp.zeros_like(halo))
    pad_scr[0, CONV_HALO:CONV_HALO + tc, :] = u_ref[0]
    span = pad_scr.shape[1] - SUBLANES
    for r in range(1, SUBLANES):
        pad_scr[r, 0:span, :] = pad_scr[0, r:r + span, :]

    @pl.when((pl.program_id(0) == 0) & (pl.program_id(1) == 0))
    def _():
        for j in range(width):
            taps_scr[j] = jnp.broadcast_to(dw_ref[j:j + 1, :], taps_scr.shape[1:])

    first = CONV_HALO - (width - 1)
    d = conv_scr.shape[1]
    for r0 in range(0, tc, rows):
        acc = None
        for j in range(width):
            shift, base = (first + j) % SUBLANES, (first + j) // SUBLANES * SUBLANES
            window = pad_scr[shift, r0 + base:r0 + base + rows, :]
            term = taps_scr[j][None] * window.reshape(rows // SUBLANES, SUBLANES, d)
            acc = term if acc is None else acc + term
        conv_scr[r0:r0 + rows, :] = acc.reshape(rows, d)
    hc = conv_scr[...] + dwb_ref[...]
    mu = jnp.mean(hc, axis=-1, keepdims=True)
    c = hc - mu
    var = jnp.mean(c * c, axis=-1, keepdims=True)
    hn = c * lax.rsqrt(var + EPS) * lw_ref[...] + lb_ref[...]
    act = (hn * jax.nn.sigmoid(hn)).astype(BF16)
    o_ref[0] = x_ref[0] + _dot(act, w2_ref[...]) + b2_ref[...]


def conv_tail(u, dw_w, dw_b, ln_w, ln_b, w2, b2, x, *, layer, tc, rows=16):
    b, s, d = u.shape
    per_tile = tc // CONV_HALO
    tok = lambda bi, n: (bi, n, 0)
    return pl.pallas_call(
        functools.partial(_conv_tail_kernel, rows=rows),
        grid=(b, s // tc),
        in_specs=[pl.BlockSpec((1, tc, d), tok),
                  pl.BlockSpec((1, CONV_HALO, d),
                               lambda bi, n: (bi, jnp.maximum(n * per_tile - 1, 0), 0)),
                  _resident(dw_w.shape), _resident((1, d)), _resident((1, d)), _resident((1, d)),
                  _layer(w2, layer), _resident((1, d)),
                  pl.BlockSpec((1, tc, d), tok)],
        out_specs=pl.BlockSpec((1, tc, d), tok),
        out_shape=jax.ShapeDtypeStruct((b, s, d), F32),
        scratch_shapes=[pltpu.VMEM((SUBLANES, CONV_HALO + tc, d), F32), pltpu.VMEM((tc, d), F32),
                        pltpu.VMEM((dw_w.shape[0], SUBLANES, d), F32)],
        compiler_params=_cparams(2),
        name="conv_tail",
    )(u, u, dw_w, dw_b, ln_w, ln_b, w2, b2, x)


LOG2E = 1.4426950408889634


def _sb_block(q, kb, u2, carry, causal):
    tk = kb.shape[0]

    def mask_own_rows(t):
        own = jnp.where(causal, t[:tk], 0.0)
        return own if t.shape[0] == tk else jnp.concatenate([own, t[tk:]], axis=0)

    z = _dot_nt(q, kb)
    sp = jnp.maximum(z, 0.0) + jnp.log(1.0 + jnp.exp2(-jnp.abs(z))) * LOG2E
    if causal is not None:
        sp = mask_own_rows(sp)
    r = _dot(sp.astype(BF16), u2)
    a = jnp.exp2(z - r - jnp.concatenate([carry] * (tk // carry.shape[1]), axis=1))
    if causal is not None:
        a = mask_own_rows(a)
    return a.astype(BF16), jnp.broadcast_to(r[:, 0:1], carry.shape)


def _sb_attn_kernel(q_ref, k_ref, v_ref, qg_ref, kg_ref, u2_ref, o_ref,
                    kn_scr, vn_scr, qn_scr, acc_scr, car_scr, *, dh, tk, mrows):
    i = pl.program_id(2)
    tq = q_ref.shape[1]
    nsub = tq // tk
    pair = 128 // dh

    def head_lanes(t, e):
        lane = lax.broadcasted_iota(jnp.int32, t.shape, 1)
        return jnp.where(lane // dh == e, t, 0.0)

    def head_rms(t, e, gain):
        te = head_lanes(t, e)
        ms = jnp.sum(te * te, axis=-1, keepdims=True) * (1.0 / dh)
        return te * lax.rsqrt(ms + EPS) * gain

    @pl.when(i == 0)
    def _():
        for e in range(pair):
            kn_scr[e] = head_rms(k_ref[0], e, kg_ref[...]).astype(BF16)
            vn_scr[e] = head_lanes(v_ref[0], e).astype(BF16)

    for e in range(pair):
        qn_scr[e] = (head_rms(q_ref[0], e, qg_ref[...]) * (LOG2E * dh ** -0.5)).astype(BF16)
    acc_scr[...] = jnp.zeros_like(acc_scr)
    car_scr[...] = jnp.zeros_like(car_scr)

    u2 = u2_ref[...]
    row = lax.broadcasted_iota(jnp.int32, (tk, tk), 0)
    col = lax.broadcasted_iota(jnp.int32, (tk, tk), 1)
    causal = col < row

    def step(start, lo_row, mask):
        vv = jnp.concatenate([vn_scr[e, pl.ds(start, tk), :] for e in range(pair)], axis=0)
        for r0 in range(lo_row, tq, mrows):
            r1 = min(r0 + mrows, tq)
            weights = []
            for e in range(pair):
                a, total = _sb_block(qn_scr[e, r0:r1, :], kn_scr[e, pl.ds(start, tk), :], u2,
                                     car_scr[e, r0:r1, :], mask if r0 == lo_row else None)
                car_scr[e, r0:r1, :] += total
                weights.append(a)
            acc_scr[r0:r1, :] += _dot(jnp.concatenate(weights, axis=1), vv)

    for c in reversed(range(nsub)):
        step(pl.multiple_of(i * tq + c * tk, tk), c * tk, causal)

    def body(st, _):
        for c in reversed(range(nsub)):
            step(pl.multiple_of(((i - 1 - st) * nsub + c) * tk, tk), 0, None)
        return 0

    lax.fori_loop(0, i, body, 0)
    o_ref[0] = acc_scr[...].astype(o_ref.dtype)


def sb_attention(proj, qg, kg, *, tq, tk, mrows):
    b, s, d3 = proj.shape
    d = d3 // 3
    dh = d // SB_HEADS
    pair = 128 // dh
    lane_blocks = d // 128
    idx = jnp.arange(tk)
    u2 = (idx[:, None] >= idx[None, :]).astype(BF16)
    return pl.pallas_call(
        functools.partial(_sb_attn_kernel, dh=dh, tk=tk, mrows=mrows),
        grid=(b, lane_blocks, s // tq),
        in_specs=[pl.BlockSpec((1, tq, 128), lambda bi, hp, i: (bi, i, hp)),
                  pl.BlockSpec((1, s, 128), lambda bi, hp, i: (bi, 0, lane_blocks + hp)),
                  pl.BlockSpec((1, s, 128), lambda bi, hp, i: (bi, 0, 2 * lane_blocks + hp)),
                  _resident((1, 128)), _resident((1, 128)), _resident((tk, tk))],
        out_specs=pl.BlockSpec((1, tq, 128), lambda bi, hp, i: (bi, i, hp)),
        out_shape=jax.ShapeDtypeStruct((b, s, d), BF16),
        scratch_shapes=[pltpu.VMEM((pair, s, 128), BF16), pltpu.VMEM((pair, s, 128), BF16),
                        pltpu.VMEM((pair, tq, 128), BF16), pltpu.VMEM((tq, 128), F32),
                        pltpu.VMEM((pair, tq, 128), F32)],
        compiler_params=_cparams(3),
        name="sb_attn",
    )(proj, proj, proj, jnp.tile(qg, (1, pair)), jnp.tile(kg, (1, pair)), u2)


def _rope_tables(seq, dk):
    half = dk // 2
    inv_freq = ROPE_BASE ** (-jnp.arange(half, dtype=F32) / half)
    ang = jnp.arange(seq, dtype=F32)[:, None] * inv_freq[None, :]
    return jnp.cos(ang), jnp.sin(ang)


def kernel(x, norm_mix, norm_ffn, ret_w_in, ret_q_norm, ret_k_norm, ret_gn_w, ret_gn_b, ret_w_out,
           conv_pw1_w, conv_pw1_b, conv_dw_w, conv_dw_b, conv_ln_w, conv_ln_b, conv_pw2_w, conv_pw2_b,
           sb_w_in, sb_q_norm, sb_k_norm, sb_w_out, ffn_w1, ffn_w2, final_norm):
    b, s, d = x.shape
    depth = norm_mix.shape[0]
    t = b * s
    tm = min(512, s)
    cos, sin = _rope_tables(s, d // RET_HEADS)
    zero_bias = jnp.zeros((1, d), F32)
    row = lambda a: a.reshape(1, -1)
    ret_w_in, ret_w_out, conv_pw1_w, conv_pw2_w, sb_w_in, sb_w_out, ffn_w1, ffn_w2 = (
        w.astype(BF16) for w in (ret_w_in, ret_w_out, conv_pw1_w, conv_pw2_w, sb_w_in, sb_w_out,
                                 ffn_w1, ffn_w2))

    x2 = x.reshape(t, d)
    for i in range(depth):
        kind = i % N_MIXERS
        j = i // N_MIXERS
        nw = row(norm_mix[i])
        if kind == 0:
            q, k, v, g = ret_proj(x2, nw, ret_w_in, row(ret_q_norm[j]), row(ret_k_norm[j]),
                                  cos, sin, layer=j, tm=tm, seq=s)
            z = ret_core(q.reshape(b, s, d), k.reshape(b, s, d), v.reshape(b, s, 2 * d),
                         g.reshape(b, s, 2 * d), row(ret_gn_w[j]), row(ret_gn_b[j]),
                         blk=min(256, s))
            x2 = mm_res(z.reshape(t, 2 * d), ret_w_out, zero_bias, x2, layer=j, tm=tm)
        elif kind == 1:
            u = norm_glu(x2, nw, conv_pw1_w, row(conv_pw1_b[j]), layer=j, tm=tm)
            x2 = conv_tail(u.reshape(b, s, d), conv_dw_w[j], row(conv_dw_b[j]), row(conv_ln_w[j]),
                           row(conv_ln_b[j]), conv_pw2_w, row(conv_pw2_b[j]),
                           x2.reshape(b, s, d), layer=j, tc=min(256, s)).reshape(t, d)
        else:
            proj = norm_proj(x2, nw, sb_w_in, layer=j, tm=tm)
            y = sb_attention(proj.reshape(b, s, 3 * d), row(sb_q_norm[j]), row(sb_k_norm[j]),
                             tq=min(1024, s), tk=min(256, s), mrows=min(512, s))
            x2 = mm_res(y.reshape(t, d), sb_w_out, zero_bias, x2, layer=j, tm=tm)
        x2 = ffn(x2, row(norm_ffn[i]), ffn_w1, ffn_w2, row(final_norm),
                 layer=i, tm=tm, final_norm=(i == depth - 1))
    return x2.reshape(b, s, d)
```

```python
import functools

import jax
import jax.numpy as jnp
from jax import lax
from jax.experimental import pallas as pl
from jax.experimental.pallas import tpu as pltpu

F32 = jnp.float32
BF16 = jnp.bfloat16
EPS = 1e-6

CHUNK = 64
RET_HEADS = 4
ROPE_BASE = 10000.0
CONV_WIDTH = 31
SB_HEADS = 16
N_MIXERS = 3

V7X_VMEM_LIMIT_BYTES = 56 * 1024 * 1024
SUBLANES = 8
CONV_HALO = 32


def _cparams(n_axes):
    return pltpu.CompilerParams(
        dimension_semantics=("arbitrary",) * n_axes,
        vmem_limit_bytes=V7X_VMEM_LIMIT_BYTES)


def _resident(shape):
    zeros = (0,) * len(shape)
    return pl.BlockSpec(shape, lambda *_: zeros, pipeline_mode=pl.Buffered(1))


def _layer(stack, layer):
    return pl.BlockSpec((None,) + stack.shape[1:], lambda *_: (layer, 0, 0),
                        pipeline_mode=pl.Buffered(1))


def _rms(x, g):
    return x * lax.rsqrt(jnp.mean(x * x, axis=-1, keepdims=True) + EPS) * g


def _dot(a, b):
    return jnp.dot(a, b, preferred_element_type=F32)


def _dot_nt(a, b):
    return lax.dot_general(a, b, (((1,), (1,)), ((), ())), preferred_element_type=F32)


def _dot_tn(a, b):
    return lax.dot_general(a, b, (((0,), (0,)), ((), ())), preferred_element_type=F32)


def _norm_proj_kernel(x_ref, nw_ref, w_ref, head_ref, tail_ref, *, chunk):
    xn = _rms(x_ref[...], nw_ref[...]).astype(BF16)
    n_head = head_ref.shape[1]
    for c in range(0, n_head, chunk):
        head_ref[:, c:c + chunk] = _dot(xn, w_ref[:, c:c + chunk])
    for c in range(0, tail_ref.shape[1], chunk):
        tail_ref[:, c:c + chunk] = _dot(xn, w_ref[:, n_head + c:n_head + c + chunk]).astype(BF16)


def norm_proj(x, nw, w, *, layer, tm, n_f32, chunk=512):
    t, d = x.shape
    n = w.shape[2]
    row = lambda i: (i, 0)
    return pl.pallas_call(
        functools.partial(_norm_proj_kernel, chunk=chunk),
        grid=(t // tm,),
        in_specs=[pl.BlockSpec((tm, d), row), _resident((1, d)), _layer(w, layer)],
        out_specs=[pl.BlockSpec((tm, n_f32), row), pl.BlockSpec((tm, n - n_f32), row)],
        out_shape=[jax.ShapeDtypeStruct((t, n_f32), F32), jax.ShapeDtypeStruct((t, n - n_f32), BF16)],
        compiler_params=_cparams(1),
        name="norm_proj",
    )(x, nw, w)


def _norm_glu_kernel(x_ref, nw_ref, w_ref, b_ref, o_ref, *, chunk):
    d = o_ref.shape[1]
    xn = _rms(x_ref[...], nw_ref[...]).astype(BF16)
    for c in range(0, d, chunk):
        a = _dot(xn, w_ref[:, c:c + chunk]) + b_ref[:, c:c + chunk]
        gate = _dot(xn, w_ref[:, d + c:d + c + chunk]) + b_ref[:, d + c:d + c + chunk]
        o_ref[:, c:c + chunk] = a * jax.nn.sigmoid(gate)


def norm_glu(x, nw, w, b, *, layer, tm, chunk=512):
    t, d = x.shape
    return pl.pallas_call(
        functools.partial(_norm_glu_kernel, chunk=chunk),
        grid=(t // tm,),
        in_specs=[pl.BlockSpec((tm, d), lambda i: (i, 0)),
                  _resident((1, d)), _layer(w, layer), _resident((1, 2 * d))],
        out_specs=pl.BlockSpec((tm, d), lambda i: (i, 0)),
        out_shape=jax.ShapeDtypeStruct((t, d), F32),
        compiler_params=_cparams(1),
        name="norm_glu",
    )(x, nw, w, b)


def _ffn_kernel(*refs, chunk, final_norm, has_mixer_out):
    if has_mixer_out:
        y_ref, wo_ref, x_ref, nw_ref, w1_ref, w2_ref, fw_ref, o_ref, h_scr = refs
        x = x_ref[...] + _dot(y_ref[...], wo_ref[...])
    else:
        x_ref, nw_ref, w1_ref, w2_ref, fw_ref, o_ref, h_scr = refs
        x = x_ref[...]
    xn = _rms(x, nw_ref[...]).astype(BF16)
    for c in range(0, h_scr.shape[1], chunk):
        h = jnp.maximum(_dot(xn, w1_ref[:, c:c + chunk]), 0.0)
        h_scr[:, c:c + chunk] = (h * h).astype(BF16)
    y = x + _dot(h_scr[...], w2_ref[...])
    if final_norm:
        y = _rms(y, fw_ref[...])
    o_ref[...] = y


def ffn(x, nw, w1, w2, fw, *, layer, tm, final_norm, mixer_out=None, chunk=1024):
    t, d = x.shape
    dff = w1.shape[2]
    row = lambda i: (i, 0)
    args, specs = [], []
    if mixer_out is not None:
        y, wo, wo_layer = mixer_out
        args += [y, wo]
        specs += [pl.BlockSpec((tm, y.shape[1]), row), _layer(wo, wo_layer)]
    args += [x, nw, w1, w2, fw]
    specs += [pl.BlockSpec((tm, d), row), _resident((1, d)), _layer(w1, layer), _layer(w2, layer),
              _resident((1, d))]
    return pl.pallas_call(
        functools.partial(_ffn_kernel, chunk=chunk, final_norm=final_norm,
                          has_mixer_out=mixer_out is not None),
        grid=(t // tm,),
        in_specs=specs,
        out_specs=pl.BlockSpec((tm, d), row),
        out_shape=jax.ShapeDtypeStruct((t, d), F32),
        scratch_shapes=[pltpu.VMEM((tm, dff), BF16)],
        compiler_params=_cparams(1),
        name="ffn",
    )(*args)


def _ret_proj_kernel(x_ref, nw_ref, w_ref, qg_ref, kg_ref, cos_ref, sin_ref,
                     q_ref, k_ref, v_ref, g_ref, *, heads, chunk):
    d = q_ref.shape[1]
    dk = d // heads
    half = dk // 2
    xn = _rms(x_ref[...], nw_ref[...]).astype(BF16)
    cos = cos_ref[...]
    sin = sin_ref[...]
    for base, gain_ref, o_ref, scale in ((0, qg_ref, q_ref, 1.0), (d, kg_ref, k_ref, dk ** -0.5)):
        for h in range(heads):
            y = _rms(_dot(xn, w_ref[:, base + h * dk:base + (h + 1) * dk]), gain_ref[...])
            t1 = y[:, :half]
            t2 = y[:, half:]
            o_ref[:, h * dk:h * dk + half] = ((t1 * cos - t2 * sin) * scale).astype(BF16)
            o_ref[:, h * dk + half:(h + 1) * dk] = ((t1 * sin + t2 * cos) * scale).astype(BF16)
    for c in range(0, 2 * d, chunk):
        v_ref[:, c:c + chunk] = _dot(xn, w_ref[:, 2 * d + c:2 * d + c + chunk]).astype(BF16)
        gate = _dot(xn, w_ref[:, 4 * d + c:4 * d + c + chunk])
        g_ref[:, c:c + chunk] = (gate * jax.nn.sigmoid(gate)).astype(BF16)


def ret_proj(x, nw, w, qg, kg, cos, sin, *, layer, tm, seq, chunk=512):
    t, d = x.shape
    spb = seq // tm
    row = lambda i: (i, 0)
    return pl.pallas_call(
        functools.partial(_ret_proj_kernel, heads=RET_HEADS, chunk=chunk),
        grid=(t // tm,),
        in_specs=[pl.BlockSpec((tm, d), row),
                  _resident((1, d)), _layer(w, layer),
                  _resident(qg.shape), _resident(kg.shape),
                  pl.BlockSpec((tm, cos.shape[1]), lambda i: (i % spb, 0)),
                  pl.BlockSpec((tm, sin.shape[1]), lambda i: (i % spb, 0))],
        out_specs=[pl.BlockSpec((tm, d), row), pl.BlockSpec((tm, d), row),
                   pl.BlockSpec((tm, 2 * d), row), pl.BlockSpec((tm, 2 * d), row)],
        out_shape=[jax.ShapeDtypeStruct((t, d), BF16), jax.ShapeDtypeStruct((t, d), BF16),
                   jax.ShapeDtypeStruct((t, 2 * d), BF16), jax.ShapeDtypeStruct((t, 2 * d), BF16)],
        compiler_params=_cparams(1),
        name="ret_proj",
    )(x, nw, w, qg, kg, cos, sin)


def _ret_core_kernel(q_ref, k_ref, v_ref, g_ref, dm_ref, qd_ref, kd_ref, cd_ref,
                     gw_ref, gb_ref, z_ref, state):
    @pl.when(pl.program_id(1) == 0)
    def _():
        state[...] = jnp.zeros_like(state)

    heads, dk, dv = state.shape
    for h in range(heads):
        q = q_ref[0, :, h * dk:(h + 1) * dk]
        k = k_ref[0, :, h * dk:(h + 1) * dk]
        v = v_ref[0, :, h * dv:(h + 1) * dv]
        st = state[h]
        p = (_dot_nt(q, k) * dm_ref[h]).astype(BF16)
        y = _dot(p, v) + qd_ref[h] * _dot(q, st.astype(BF16))
        kk = (k.astype(F32) * kd_ref[h]).astype(BF16)
        state[h] = st * cd_ref[h] + _dot_tn(kk, v)

        mu = jnp.mean(y, axis=-1, keepdims=True)
        yc = y - mu
        var = jnp.mean(yc * yc, axis=-1, keepdims=True)
        yn = (yc * lax.rsqrt(var + EPS) * gw_ref[:, h * dv:(h + 1) * dv]
              + gb_ref[:, h * dv:(h + 1) * dv])
        silu_gate = g_ref[0, :, h * dv:(h + 1) * dv].astype(F32)
        z_ref[0, :, h * dv:(h + 1) * dv] = (silu_gate * yn).astype(BF16)


def _ret_decay_tables(blk):
    h = jnp.arange(RET_HEADS, dtype=F32)
    log_g = jnp.log(1.0 - jnp.exp2(-5.0 - h))
    idx = jnp.arange(blk, dtype=F32)
    dist = idx[:, None] - idx[None, :]
    ct = jnp.arange(blk)[:, None] // CHUNK
    cs = jnp.arange(blk)[None, :] // CHUNK
    expo = jnp.where(ct == cs, jnp.abs(dist), dist)
    dm = jnp.where((cs <= ct)[None], jnp.exp(log_g[:, None, None] * expo[None]), 0.0)
    qd = jnp.exp(log_g[:, None] * (idx + 1.0))[..., None]
    kd = jnp.exp(log_g[:, None] * (blk - 1.0 - idx))[..., None]
    cd = jnp.exp(log_g * blk)[:, None, None]
    return dm, qd, kd, cd


def ret_core(q, k, v, g, gn_w, gn_b, *, blk):
    b, s, d = q.shape
    dk = d // RET_HEADS
    dv = 2 * d // RET_HEADS
    dm, qd, kd, cd = _ret_decay_tables(blk)
    tok = lambda bi, n: (bi, n, 0)
    return pl.pallas_call(
        _ret_core_kernel,
        grid=(b, s // blk),
        in_specs=[pl.BlockSpec((1, blk, d), tok), pl.BlockSpec((1, blk, d), tok),
                  pl.BlockSpec((1, blk, 2 * d), tok), pl.BlockSpec((1, blk, 2 * d), tok),
                  _resident(dm.shape), _resident(qd.shape), _resident(kd.shape), _resident(cd.shape),
                  _resident((1, 2 * d)), _resident((1, 2 * d))],
        out_specs=pl.BlockSpec((1, blk, 2 * d), tok),
        out_shape=jax.ShapeDtypeStruct((b, s, 2 * d), BF16),
        scratch_shapes=[pltpu.VMEM((RET_HEADS, dk, dv), F32)],
        compiler_params=_cparams(2),
        name="ret_core",
    )(q, k, v, g, dm, qd, kd, cd, gn_w, gn_b)


def _conv_tail_kernel(u_ref, halo_ref, dw_ref, dwb_ref, lw_ref, lb_ref, w2_ref, b2_ref, x_ref,
                      o_ref, pad_scr, conv_scr, taps_scr, *, rows):
    tc = u_ref.shape[1]
    width = dw_ref.shape[0]
    halo = halo_ref[0]
    pad_scr[0, 0:CONV_HALO, :] = jnp.where(pl.program_id(1) > 0, halo, jnp.zeros_like(halo))
    pad_scr[0, CONV_HALO:CONV_HALO + tc, :] = u_ref[0]
    span = pad_scr.shape[1] - SUBLANES
    for r in range(1, SUBLANES):
        pad_scr[r, 0:span, :] = pad_scr[0, r:r + span, :]

    @pl.when((pl.program_id(0) == 0) & (pl.program_id(1) == 0))
    def _():
        for j in range(width):
            taps_scr[j] = jnp.broadcast_to(dw_ref[j:j + 1, :], taps_scr.shape[1:])

    first = CONV_HALO - (width - 1)
    d = conv_scr.shape[1]
    for r0 in range(0, tc, rows):
        acc = None
        for j in range(width):
            shift, base = (first + j) % SUBLANES, (first + j) // SUBLANES * SUBLANES
            window = pad_scr[shift, r0 + base:r0 + base + rows, :]
            term = taps_scr[j][None] * window.reshape(rows // SUBLANES, SUBLANES, d)
            acc = term if acc is None else acc + term
        conv_scr[r0:r0 + rows, :] = acc.reshape(rows, d)
    hc = conv_scr[...] + dwb_ref[...]
    mu = jnp.mean(hc, axis=-1, keepdims=True)
    c = hc - mu
    var = jnp.mean(c * c, axis=-1, keepdims=True)
    hn = c * lax.rsqrt(var + EPS) * lw_ref[...] + lb_ref[...]
    act = (hn * jax.nn.sigmoid(hn)).astype(BF16)
    o_ref[0] = x_ref[0] + _dot(act, w2_ref[...]) + b2_ref[...]


def conv_tail(u, dw_w, dw_b, ln_w, ln_b, w2, b2, x, *, layer, tc, rows=16):
    b, s, d = u.shape
    per_tile = tc // CONV_HALO
    tok = lambda bi, n: (bi, n, 0)
    return pl.pallas_call(
        functools.partial(_conv_tail_kernel, rows=rows),
        grid=(b, s // tc),
        in_specs=[pl.BlockSpec((1, tc, d), tok),
                  pl.BlockSpec((1, CONV_HALO, d),
                               lambda bi, n: (bi, jnp.maximum(n * per_tile - 1, 0), 0)),
                  _resident(dw_w.shape), _resident((1, d)), _resident((1, d)), _resident((1, d)),
                  _layer(w2, layer), _resident((1, d)),
                  pl.BlockSpec((1, tc, d), tok)],
        out_specs=pl.BlockSpec((1, tc, d), tok),
        out_shape=jax.ShapeDtypeStruct((b, s, d), F32),
        scratch_shapes=[pltpu.VMEM((SUBLANES, CONV_HALO + tc, d), F32), pltpu.VMEM((tc, d), F32),
                        pltpu.VMEM((dw_w.shape[0], SUBLANES, d), F32)],
        compiler_params=_cparams(2),
        name="conv_tail",
    )(u, u, dw_w, dw_b, ln_w, ln_b, w2, b2, x)


LOG2E = 1.4426950408889634


def _sb_block(q, kb, u2, carry, causal):
    tk = kb.shape[0]

    def mask_own_rows(t):
        own = jnp.where(causal, t[:tk], 0.0)
        return own if t.shape[0] == tk else jnp.concatenate([own, t[tk:]], axis=0)

    z = _dot_nt(q, kb)
    sp = jnp.maximum(z, 0.0) + jnp.log(1.0 + jnp.exp2(-jnp.abs(z))) * LOG2E
    if causal is not None:
        sp = mask_own_rows(sp)
    r = _dot(sp.astype(BF16), u2)
    a = jnp.exp2(z - r - jnp.concatenate([carry] * (tk // carry.shape[1]), axis=1))
    if causal is not None:
        a = mask_own_rows(a)
    return a.astype(BF16), jnp.broadcast_to(r[:, 0:1], carry.shape)


def _sb_attn_kernel(q_ref, k_ref, v_ref, qg_ref, kg_ref, u2_ref, o_ref,
                    kn_scr, vn_scr, qn_scr, acc_scr, car_scr, *, dh, tk, mrows):
    i = pl.program_id(2)
    tq = q_ref.shape[1]
    nsub = tq // tk
    pair = 128 // dh

    def head_lanes(t, e):
        lane = lax.broadcasted_iota(jnp.int32, t.shape, 1)
        return jnp.where(lane // dh == e, t, 0.0)

    def head_rms(t, e, gain):
        te = head_lanes(t, e)
        ms = jnp.sum(te * te, axis=-1, keepdims=True) * (1.0 / dh)
        return te * lax.rsqrt(ms + EPS) * gain

    @pl.when(i == 0)
    def _():
        for e in range(pair):
            kn_scr[e] = head_rms(k_ref[0], e, kg_ref[...]).astype(BF16)
            vn_scr[e] = head_lanes(v_ref[0].astype(F32), e).astype(BF16)

    for e in range(pair):
        qn_scr[e] = (head_rms(q_ref[0], e, qg_ref[...]) * (LOG2E * dh ** -0.5)).astype(BF16)
    acc_scr[...] = jnp.zeros_like(acc_scr)
    car_scr[...] = jnp.zeros_like(car_scr)

    u2 = u2_ref[...]
    row = lax.broadcasted_iota(jnp.int32, (tk, tk), 0)
    col = lax.broadcasted_iota(jnp.int32, (tk, tk), 1)
    causal = col < row

    def step(start, lo_row, mask):
        vv = jnp.concatenate([vn_scr[e, pl.ds(start, tk), :] for e in range(pair)], axis=0)
        for r0 in range(lo_row, tq, mrows):
            r1 = min(r0 + mrows, tq)
            weights = []
            for e in range(pair):
                a, total = _sb_block(qn_scr[e, r0:r1, :], kn_scr[e, pl.ds(start, tk), :], u2,
                                     car_scr[e, r0:r1, :], mask if r0 == lo_row else None)
                car_scr[e, r0:r1, :] += total
                weights.append(a)
            acc_scr[r0:r1, :] += _dot(jnp.concatenate(weights, axis=1), vv)

    for c in reversed(range(nsub)):
        step(pl.multiple_of(i * tq + c * tk, tk), c * tk, causal)

    def body(st, _):
        for c in reversed(range(nsub)):
            step(pl.multiple_of(((i - 1 - st) * nsub + c) * tk, tk), 0, None)
        return 0

    lax.fori_loop(0, i, body, 0)
    o_ref[0] = acc_scr[...].astype(o_ref.dtype)


def sb_attention(qk, v, qg, kg, *, tq, tk, mrows):
    b, s, d = v.shape
    dh = d // SB_HEADS
    pair = 128 // dh
    lane_blocks = d // 128
    idx = jnp.arange(tk)
    u2 = (idx[:, None] >= idx[None, :]).astype(BF16)
    return pl.pallas_call(
        functools.partial(_sb_attn_kernel, dh=dh, tk=tk, mrows=mrows),
        grid=(b, lane_blocks, s // tq),
        in_specs=[pl.BlockSpec((1, tq, 128), lambda bi, hp, i: (bi, i, hp)),
                  pl.BlockSpec((1, s, 128), lambda bi, hp, i: (bi, 0, lane_blocks + hp)),
                  pl.BlockSpec((1, s, 128), lambda bi, hp, i: (bi, 0, hp)),
                  _resident((1, 128)), _resident((1, 128)), _resident((tk, tk))],
        out_specs=pl.BlockSpec((1, tq, 128), lambda bi, hp, i: (bi, i, hp)),
        out_shape=jax.ShapeDtypeStruct((b, s, d), BF16),
        scratch_shapes=[pltpu.VMEM((pair, s, 128), BF16), pltpu.VMEM((pair, s, 128), BF16),
                        pltpu.VMEM((pair, tq, 128), BF16), pltpu.VMEM((tq, 128), F32),
                        pltpu.VMEM((pair, tq, 128), F32)],
        compiler_params=_cparams(3),
        name="sb_attn",
    )(qk, qk, v, jnp.tile(qg, (1, pair)), jnp.tile(kg, (1, pair)), u2)


def _rope_tables(seq, dk):
    half = dk // 2
    inv_freq = ROPE_BASE ** (-jnp.arange(half, dtype=F32) / half)
    ang = jnp.arange(seq, dtype=F32)[:, None] * inv_freq[None, :]
    return jnp.cos(ang), jnp.sin(ang)


def kernel(x, norm_mix, norm_ffn, ret_w_in, ret_q_norm, ret_k_norm, ret_gn_w, ret_gn_b, ret_w_out,
           conv_pw1_w, conv_pw1_b, conv_dw_w, conv_dw_b, conv_ln_w, conv_ln_b, conv_pw2_w, conv_pw2_b,
           sb_w_in, sb_q_norm, sb_k_norm, sb_w_out, ffn_w1, ffn_w2, final_norm):
    b, s, d = x.shape
    depth = norm_mix.shape[0]
    t = b * s
    tm = min(512, s)
    cos, sin = _rope_tables(s, d // RET_HEADS)
    row = lambda a: a.reshape(1, -1)
    ret_w_in, ret_w_out, conv_pw1_w, conv_pw2_w, sb_w_in, sb_w_out, ffn_w1, ffn_w2 = (
        w.astype(BF16) for w in (ret_w_in, ret_w_out, conv_pw1_w, conv_pw2_w, sb_w_in, sb_w_out,
                                 ffn_w1, ffn_w2))

    x2 = x.reshape(t, d)
    for i in range(depth):
        kind = i % N_MIXERS
        j = i // N_MIXERS
        nw = row(norm_mix[i])
        mixer_out = None
        if kind == 0:
            q, k, v, g = ret_proj(x2, nw, ret_w_in, row(ret_q_norm[j]), row(ret_k_norm[j]),
                                  cos, sin, layer=j, tm=tm, seq=s)
            z = ret_core(q.reshape(b, s, d), k.reshape(b, s, d), v.reshape(b, s, 2 * d),
                         g.reshape(b, s, 2 * d), row(ret_gn_w[j]), row(ret_gn_b[j]),
                         blk=min(256, s))
            mixer_out = (z.reshape(t, 2 * d), ret_w_out, j)
        elif kind == 1:
            u = norm_glu(x2, nw, conv_pw1_w, row(conv_pw1_b[j]), layer=j, tm=tm)
            x2 = conv_tail(u.reshape(b, s, d), conv_dw_w[j], row(conv_dw_b[j]), row(conv_ln_w[j]),
                           row(conv_ln_b[j]), conv_pw2_w, row(conv_pw2_b[j]),
                           x2.reshape(b, s, d), layer=j, tc=min(256, s)).reshape(t, d)
        else:
            qk, v = norm_proj(x2, nw, sb_w_in, layer=j, tm=tm, n_f32=2 * d)
            y = sb_attention(qk.reshape(b, s, 2 * d), v.reshape(b, s, d), row(sb_q_norm[j]),
                             row(sb_k_norm[j]), tq=min(1024, s), tk=min(256, s), mrows=min(512, s))
            mixer_out = (y.reshape(t, d), sb_w_out, j)
        x2 = ffn(x2, row(norm_ffn[i]), ffn_w1, ffn_w2, row(final_norm), layer=i, tm=tm,
                 final_norm=(i == depth - 1), mixer_out=mixer_out)
    return x2.reshape(b, s, d)
```

```python
import functools

import jax
import jax.numpy as jnp
from jax import lax
from jax.experimental import pallas as pl
from jax.experimental.pallas import tpu as pltpu

F32 = jnp.float32
BF16 = jnp.bfloat16
EPS = 1e-6

CHUNK = 64
RET_HEADS = 4
ROPE_BASE = 10000.0
CONV_WIDTH = 31
SB_HEADS = 16
N_MIXERS = 3

V7X_VMEM_LIMIT_BYTES = 56 * 1024 * 1024
SUBLANES = 8
CONV_HALO = 32


def _cparams(n_axes):
    return pltpu.CompilerParams(
        dimension_semantics=("arbitrary",) * n_axes,
        vmem_limit_bytes=V7X_VMEM_LIMIT_BYTES)


def _resident(shape):
    zeros = (0,) * len(shape)
    return pl.BlockSpec(shape, lambda *_: zeros, pipeline_mode=pl.Buffered(1))


def _layer(stack, layer):
    return pl.BlockSpec((None,) + stack.shape[1:], lambda *_: (layer, 0, 0),
                        pipeline_mode=pl.Buffered(1))


def _rms(x, g):
    return x * lax.rsqrt(jnp.mean(x * x, axis=-1, keepdims=True) + EPS) * g


def _dot(a, b):
    return jnp.dot(a, b, preferred_element_type=F32)


def _dot_nt(a, b):
    return lax.dot_general(a, b, (((1,), (1,)), ((), ())), preferred_element_type=F32)


def _dot_tn(a, b):
    return lax.dot_general(a, b, (((0,), (0,)), ((), ())), preferred_element_type=F32)


def _norm_proj_kernel(x_ref, nw_ref, w_ref, head_ref, tail_ref, *, chunk):
    xn = _rms(x_ref[...], nw_ref[...]).astype(BF16)
    n_head = head_ref.shape[1]
    for c in range(0, n_head, chunk):
        head_ref[:, c:c + chunk] = _dot(xn, w_ref[:, c:c + chunk])
    for c in range(0, tail_ref.shape[1], chunk):
        tail_ref[:, c:c + chunk] = _dot(xn, w_ref[:, n_head + c:n_head + c + chunk]).astype(BF16)


def norm_proj(x, nw, w, *, layer, tm, n_f32, chunk=512):
    t, d = x.shape
    n = w.shape[2]
    row = lambda i: (i, 0)
    return pl.pallas_call(
        functools.partial(_norm_proj_kernel, chunk=chunk),
        grid=(t // tm,),
        in_specs=[pl.BlockSpec((tm, d), row), _resident((1, d)), _layer(w, layer)],
        out_specs=[pl.BlockSpec((tm, n_f32), row), pl.BlockSpec((tm, n - n_f32), row)],
        out_shape=[jax.ShapeDtypeStruct((t, n_f32), F32), jax.ShapeDtypeStruct((t, n - n_f32), BF16)],
        compiler_params=_cparams(1),
        name="norm_proj",
    )(x, nw, w)


def _ffn_kernel(*refs, chunk, final_norm, has_mixer_out):
    if has_mixer_out:
        y_ref, wo_ref, x_ref, nw_ref, w1_ref, w2_ref, fw_ref, o_ref, h_scr = refs
        x = x_ref[...] + _dot(y_ref[...], wo_ref[...])
    else:
        x_ref, nw_ref, w1_ref, w2_ref, fw_ref, o_ref, h_scr = refs
        x = x_ref[...]
    xn = _rms(x, nw_ref[...]).astype(BF16)
    for c in range(0, h_scr.shape[1], chunk):
        h = jnp.maximum(_dot(xn, w1_ref[:, c:c + chunk]), 0.0)
        h_scr[:, c:c + chunk] = (h * h).astype(BF16)
    y = x + _dot(h_scr[...], w2_ref[...])
    if final_norm:
        y = _rms(y, fw_ref[...])
    o_ref[...] = y


def ffn(x, nw, w1, w2, fw, *, layer, tm, final_norm, mixer_out=None, chunk=1024):
    t, d = x.shape
    dff = w1.shape[2]
    row = lambda i: (i, 0)
    args, specs = [], []
    if mixer_out is not None:
        y, wo, wo_layer = mixer_out
        args += [y, wo]
        specs += [pl.BlockSpec((tm, y.shape[1]), row), _layer(wo, wo_layer)]
    args += [x, nw, w1, w2, fw]
    specs += [pl.BlockSpec((tm, d), row), _resident((1, d)), _layer(w1, layer), _layer(w2, layer),
              _resident((1, d))]
    return pl.pallas_call(
        functools.partial(_ffn_kernel, chunk=chunk, final_norm=final_norm,
                          has_mixer_out=mixer_out is not None),
        grid=(t // tm,),
        in_specs=specs,
        out_specs=pl.BlockSpec((tm, d), row),
        out_shape=jax.ShapeDtypeStruct((t, d), F32),
        scratch_shapes=[pltpu.VMEM((tm, dff), BF16)],
        compiler_params=_cparams(1),
        name="ffn",
    )(*args)


def _retention_kernel(x_ref, nw_ref, w_ref, qg_ref, kg_ref, cos_ref, sin_ref,
                      dm_ref, qd_ref, kd_ref, cd_ref, gw_ref, gb_ref, z_ref, state):
    @pl.when(pl.program_id(1) == 0)
    def _():
        state[...] = jnp.zeros_like(state)

    heads, dk, dv = state.shape
    d = heads * dk
    half = dk // 2
    xn = _rms(x_ref[0], nw_ref[...]).astype(BF16)
    cos = cos_ref[...]
    sin = sin_ref[...]

    def normed_rotated(col, gain_ref, scale):
        y = _rms(_dot(xn, w_ref[:, col:col + dk]), gain_ref[...])
        t1 = y[:, :half]
        t2 = y[:, half:]
        rot = jnp.concatenate([t1 * cos - t2 * sin, t1 * sin + t2 * cos], axis=1)
        return (rot * scale).astype(BF16)

    for h in range(heads):
        q = normed_rotated(h * dk, qg_ref, 1.0)
        k = normed_rotated(d + h * dk, kg_ref, dk ** -0.5)
        v = _dot(xn, w_ref[:, 2 * d + h * dv:2 * d + (h + 1) * dv]).astype(BF16)
        gate = _dot(xn, w_ref[:, 4 * d + h * dv:4 * d + (h + 1) * dv])

        st = state[h]
        p = (_dot_nt(q, k) * dm_ref[h]).astype(BF16)
        y = _dot(p, v) + qd_ref[h] * _dot(q, st.astype(BF16))
        kk = (k.astype(F32) * kd_ref[h]).astype(BF16)
        state[h] = st * cd_ref[h] + _dot_tn(kk, v)

        mu = jnp.mean(y, axis=-1, keepdims=True)
        yc = y - mu
        var = jnp.mean(yc * yc, axis=-1, keepdims=True)
        yn = (yc * lax.rsqrt(var + EPS) * gw_ref[:, h * dv:(h + 1) * dv]
              + gb_ref[:, h * dv:(h + 1) * dv])
        z_ref[0, :, h * dv:(h + 1) * dv] = (gate * jax.nn.sigmoid(gate) * yn).astype(BF16)


def _ret_decay_tables(blk):
    h = jnp.arange(RET_HEADS, dtype=F32)
    log_g = jnp.log(1.0 - jnp.exp2(-5.0 - h))
    idx = jnp.arange(blk, dtype=F32)
    dist = idx[:, None] - idx[None, :]
    ct = jnp.arange(blk)[:, None] // CHUNK
    cs = jnp.arange(blk)[None, :] // CHUNK
    expo = jnp.where(ct == cs, jnp.abs(dist), dist)
    dm = jnp.where((cs <= ct)[None], jnp.exp(log_g[:, None, None] * expo[None]), 0.0)
    qd = jnp.exp(log_g[:, None] * (idx + 1.0))[..., None]
    kd = jnp.exp(log_g[:, None] * (blk - 1.0 - idx))[..., None]
    cd = jnp.exp(log_g * blk)[:, None, None]
    return dm, qd, kd, cd


def retention(x, nw, w, qg, kg, cos, sin, gn_w, gn_b, *, layer, blk):
    b, s, d = x.shape
    dk = d // RET_HEADS
    dv = 2 * d // RET_HEADS
    dm, qd, kd, cd = _ret_decay_tables(blk)
    tok = lambda bi, n: (bi, n, 0)
    pos = lambda bi, n: (n, 0)
    return pl.pallas_call(
        _retention_kernel,
        grid=(b, s // blk),
        in_specs=[pl.BlockSpec((1, blk, d), tok), _resident((1, d)), _layer(w, layer),
                  _resident(qg.shape), _resident(kg.shape),
                  pl.BlockSpec((blk, cos.shape[1]), pos), pl.BlockSpec((blk, sin.shape[1]), pos),
                  _resident(dm.shape), _resident(qd.shape), _resident(kd.shape), _resident(cd.shape),
                  _resident((1, 2 * d)), _resident((1, 2 * d))],
        out_specs=pl.BlockSpec((1, blk, 2 * d), tok),
        out_shape=jax.ShapeDtypeStruct((b, s, 2 * d), BF16),
        scratch_shapes=[pltpu.VMEM((RET_HEADS, dk, dv), F32)],
        compiler_params=_cparams(2),
        name="retention",
    )(x, nw, w, qg, kg, cos, sin, dm, qd, kd, cd, gn_w, gn_b)


def _conv_mixer_kernel(x_ref, nw_ref, w1_ref, b1_ref, dw_ref, dwb_ref, lw_ref, lb_ref, w2_ref, b2_ref,
                       o_ref, pad_scr, conv_scr, taps_scr, *, rows, chunk):
    tc = x_ref.shape[1]
    width = dw_ref.shape[0]
    d = conv_scr.shape[1]

    @pl.when(pl.program_id(1) == 0)
    def _():
        pad_scr[0, 0:CONV_HALO, :] = jnp.zeros((CONV_HALO, d), F32)

    @pl.when(pl.program_id(1) > 0)
    def _():
        pad_scr[0, 0:CONV_HALO, :] = pad_scr[0, tc:tc + CONV_HALO, :]

    @pl.when((pl.program_id(0) == 0) & (pl.program_id(1) == 0))
    def _():
        for j in range(width):
            taps_scr[j] = jnp.broadcast_to(dw_ref[j:j + 1, :], taps_scr.shape[1:])

    x = x_ref[0]
    xn = _rms(x, nw_ref[...]).astype(BF16)
    for c in range(0, d, chunk):
        a = _dot(xn, w1_ref[:, c:c + chunk]) + b1_ref[:, c:c + chunk]
        gate = _dot(xn, w1_ref[:, d + c:d + c + chunk]) + b1_ref[:, d + c:d + c + chunk]
        pad_scr[0, CONV_HALO:CONV_HALO + tc, c:c + chunk] = a * jax.nn.sigmoid(gate)
    span = pad_scr.shape[1] - SUBLANES
    for r in range(1, SUBLANES):
        pad_scr[r, 0:span, :] = pad_scr[0, r:r + span, :]
    first = CONV_HALO - (width - 1)
    for r0 in range(0, tc, rows):
        acc = None
        for j in range(width):
            shift, base = (first + j) % SUBLANES, (first + j) // SUBLANES * SUBLANES
            window = pad_scr[shift, r0 + base:r0 + base + rows, :]
            term = taps_scr[j][None] * window.reshape(rows // SUBLANES, SUBLANES, d)
            acc = term if acc is None else acc + term
        conv_scr[r0:r0 + rows, :] = acc.reshape(rows, d)
    hc = conv_scr[...] + dwb_ref[...]
    mu = jnp.mean(hc, axis=-1, keepdims=True)
    cen = hc - mu
    var = jnp.mean(cen * cen, axis=-1, keepdims=True)
    hn = cen * lax.rsqrt(var + EPS) * lw_ref[...] + lb_ref[...]
    act = (hn * jax.nn.sigmoid(hn)).astype(BF16)
    o_ref[0] = x + _dot(act, w2_ref[...]) + b2_ref[...]


def conv_mixer(x, nw, w1, b1, dw_w, dw_b, ln_w, ln_b, w2, b2, *, layer, tc, rows=16, chunk=512):
    b, s, d = x.shape
    tok = lambda bi, n: (bi, n, 0)
    return pl.pallas_call(
        functools.partial(_conv_mixer_kernel, rows=rows, chunk=chunk),
        grid=(b, s // tc),
        in_specs=[pl.BlockSpec((1, tc, d), tok), _resident((1, d)),
                  _layer(w1, layer), _resident((1, 2 * d)),
                  _resident(dw_w.shape), _resident((1, d)), _resident((1, d)), _resident((1, d)),
                  _layer(w2, layer), _resident((1, d))],
        out_specs=pl.BlockSpec((1, tc, d), tok),
        out_shape=jax.ShapeDtypeStruct((b, s, d), F32),
        scratch_shapes=[pltpu.VMEM((SUBLANES, CONV_HALO + tc, d), F32), pltpu.VMEM((tc, d), F32),
                        pltpu.VMEM((dw_w.shape[0], SUBLANES, d), F32)],
        compiler_params=_cparams(2),
        name="conv_mixer",
    )(x, nw, w1, b1, dw_w, dw_b, ln_w, ln_b, w2, b2)


LOG2E = 1.4426950408889634
SOFTPLUS_CLAMP = 96.0


def _sb_block(q, kb, u2, carry, causal):
    tk = kb.shape[0]

    def mask_own_rows(t):
        own = jnp.where(causal, t[:tk], 0.0)
        return own if t.shape[0] == tk else jnp.concatenate([own, t[tk:]], axis=0)

    z = _dot_nt(q, kb)
    sp = jnp.maximum(jnp.log(1.0 + jnp.exp2(jnp.minimum(z, SOFTPLUS_CLAMP))) * LOG2E, z)
    if causal is not None:
        sp = mask_own_rows(sp)
    r = _dot(sp.astype(BF16), u2)
    a = jnp.exp2(z - r - jnp.concatenate([carry] * (tk // carry.shape[1]), axis=1))
    if causal is not None:
        a = mask_own_rows(a)
    return a.astype(BF16), jnp.broadcast_to(r[:, 0:1], carry.shape)


def _sb_attn_kernel(q_ref, k_ref, v_ref, qg_ref, kg_ref, u2_ref, o_ref,
                    kn_scr, vn_scr, qn_scr, acc_scr, car_scr, *, dh, tk, mrows):
    i = pl.program_id(2)
    tq = q_ref.shape[1]
    nsub = tq // tk
    pair = 128 // dh

    def head_lanes(t, e):
        lane = lax.broadcasted_iota(jnp.int32, t.shape, 1)
        return jnp.where(lane // dh == e, t, 0.0)

    def head_rms(t, e, gain):
        te = head_lanes(t, e)
        ms = jnp.sum(te * te, axis=-1, keepdims=True) * (1.0 / dh)
        return te * lax.rsqrt(ms + EPS) * gain

    @pl.when(i == 0)
    def _():
        for e in range(pair):
            kn_scr[e] = head_rms(k_ref[0], e, kg_ref[...]).astype(BF16)
            vn_scr[e] = head_lanes(v_ref[0].astype(F32), e).astype(BF16)

    for e in range(pair):
        qn_scr[e] = (head_rms(q_ref[0], e, qg_ref[...]) * (LOG2E * dh ** -0.5)).astype(BF16)
    acc_scr[...] = jnp.zeros_like(acc_scr)
    car_scr[...] = jnp.zeros_like(car_scr)

    u2 = u2_ref[...]
    row = lax.broadcasted_iota(jnp.int32, (tk, tk), 0)
    col = lax.broadcasted_iota(jnp.int32, (tk, tk), 1)
    causal = col < row

    def step(start, lo_row, mask):
        vv = jnp.concatenate([vn_scr[e, pl.ds(start, tk), :] for e in range(pair)], axis=0)
        for r0 in range(lo_row, tq, mrows):
            r1 = min(r0 + mrows, tq)
            weights = []
            for e in range(pair):
                a, total = _sb_block(qn_scr[e, r0:r1, :], kn_scr[e, pl.ds(start, tk), :], u2,
                                     car_scr[e, r0:r1, :], mask if r0 == lo_row else None)
                car_scr[e, r0:r1, :] += total
                weights.append(a)
            acc_scr[r0:r1, :] += _dot(jnp.concatenate(weights, axis=1), vv)

    for c in reversed(range(nsub)):
        step(pl.multiple_of(i * tq + c * tk, tk), c * tk, causal)

    def body(st, _):
        for c in reversed(range(nsub)):
            step(pl.multiple_of(((i - 1 - st) * nsub + c) * tk, tk), 0, None)
        return 0

    lax.fori_loop(0, i, body, 0)
    o_ref[0] = acc_scr[...].astype(o_ref.dtype)


def sb_attention(qk, v, qg, kg, *, tq, tk, mrows):
    b, s, d = v.shape
    dh = d // SB_HEADS
    pair = 128 // dh
    lane_blocks = d // 128
    idx = jnp.arange(tk)
    u2 = (idx[:, None] >= idx[None, :]).astype(BF16)
    return pl.pallas_call(
        functools.partial(_sb_attn_kernel, dh=dh, tk=tk, mrows=mrows),
        grid=(b, lane_blocks, s // tq),
        in_specs=[pl.BlockSpec((1, tq, 128), lambda bi, hp, i: (bi, i, hp)),
                  pl.BlockSpec((1, s, 128), lambda bi, hp, i: (bi, 0, lane_blocks + hp)),
                  pl.BlockSpec((1, s, 128), lambda bi, hp, i: (bi, 0, hp)),
                  _resident((1, 128)), _resident((1, 128)), _resident((tk, tk))],
        out_specs=pl.BlockSpec((1, tq, 128), lambda bi, hp, i: (bi, i, hp)),
        out_shape=jax.ShapeDtypeStruct((b, s, d), BF16),
        scratch_shapes=[pltpu.VMEM((pair, s, 128), BF16), pltpu.VMEM((pair, s, 128), BF16),
                        pltpu.VMEM((pair, tq, 128), BF16), pltpu.VMEM((tq, 128), F32),
                        pltpu.VMEM((pair, tq, 128), F32)],
        compiler_params=_cparams(3),
        name="sb_attn",
    )(qk, qk, v, jnp.tile(qg, (1, pair)), jnp.tile(kg, (1, pair)), u2)


def _rope_tables(seq, dk):
    half = dk // 2
    inv_freq = ROPE_BASE ** (-jnp.arange(half, dtype=F32) / half)
    ang = jnp.arange(seq, dtype=F32)[:, None] * inv_freq[None, :]
    return jnp.cos(ang), jnp.sin(ang)


def kernel(x, norm_mix, norm_ffn, ret_w_in, ret_q_norm, ret_k_norm, ret_gn_w, ret_gn_b, ret_w_out,
           conv_pw1_w, conv_pw1_b, conv_dw_w, conv_dw_b, conv_ln_w, conv_ln_b, conv_pw2_w, conv_pw2_b,
           sb_w_in, sb_q_norm, sb_k_norm, sb_w_out, ffn_w1, ffn_w2, final_norm):
    b, s, d = x.shape
    depth = norm_mix.shape[0]
    t = b * s
    tm = min(512, s)
    cos, sin = _rope_tables(s, d // RET_HEADS)
    row = lambda a: a.reshape(1, -1)
    ret_w_in, ret_w_out, conv_pw1_w, conv_pw2_w, sb_w_in, sb_w_out, ffn_w1, ffn_w2 = (
        w.astype(BF16) for w in (ret_w_in, ret_w_out, conv_pw1_w, conv_pw2_w, sb_w_in, sb_w_out,
                                 ffn_w1, ffn_w2))

    x2 = x.reshape(t, d)
    for i in range(depth):
        kind = i % N_MIXERS
        j = i // N_MIXERS
        nw = row(norm_mix[i])
        mixer_out = None
        if kind == 0:
            z = retention(x2.reshape(b, s, d), nw, ret_w_in, row(ret_q_norm[j]), row(ret_k_norm[j]),
                          cos, sin, row(ret_gn_w[j]), row(ret_gn_b[j]), layer=j, blk=min(256, s))
            mixer_out = (z.reshape(t, 2 * d), ret_w_out, j)
        elif kind == 1:
            x2 = conv_mixer(x2.reshape(b, s, d), nw, conv_pw1_w, row(conv_pw1_b[j]), conv_dw_w[j],
                            row(conv_dw_b[j]), row(conv_ln_w[j]), row(conv_ln_b[j]), conv_pw2_w,
                            row(conv_pw2_b[j]), layer=j, tc=min(512, s)).reshape(t, d)
        else:
            qk, v = norm_proj(x2, nw, sb_w_in, layer=j, tm=tm, n_f32=2 * d)
            y = sb_attention(qk.reshape(b, s, 2 * d), v.reshape(b, s, d), row(sb_q_norm[j]),
                             row(sb_k_norm[j]), tq=min(1024, s), tk=min(256, s), mrows=min(512, s))
            mixer_out = (y.reshape(t, d), sb_w_out, j)
        x2 = ffn(x2, row(norm_ffn[i]), ffn_w1, ffn_w2, row(final_norm), layer=i, tm=tm,
                 final_norm=(i == depth - 1), mixer_out=mixer_out)
    return x2.reshape(b, s, d)
```

```python
import functools

import jax
import jax.numpy as jnp
from jax import lax
from jax.experimental import pallas as pl
from jax.experimental.pallas import tpu as pltpu

F32 = jnp.float32
BF16 = jnp.bfloat16
EPS = 1e-6

CHUNK = 64
RET_HEADS = 4
ROPE_BASE = 10000.0
CONV_WIDTH = 31
SB_HEADS = 16
N_MIXERS = 3

V7X_VMEM_LIMIT_BYTES = 56 * 1024 * 1024
SUBLANES = 8
CONV_HALO = 32


def _cparams(n_axes):
    return pltpu.CompilerParams(
        dimension_semantics=("arbitrary",) * n_axes,
        vmem_limit_bytes=V7X_VMEM_LIMIT_BYTES)


def _resident(shape):
    zeros = (0,) * len(shape)
    return pl.BlockSpec(shape, lambda *_: zeros, pipeline_mode=pl.Buffered(1))


def _layer(stack, layer):
    return pl.BlockSpec((None,) + stack.shape[1:], lambda *_: (layer, 0, 0),
                        pipeline_mode=pl.Buffered(1))


def _rms(x, g):
    return x * lax.rsqrt(jnp.mean(x * x, axis=-1, keepdims=True) + EPS) * g


def _dot(a, b):
    return jnp.dot(a, b, preferred_element_type=F32)


def _dot_nt(a, b):
    return lax.dot_general(a, b, (((1,), (1,)), ((), ())), preferred_element_type=F32)


def _dot_tn(a, b):
    return lax.dot_general(a, b, (((0,), (0,)), ((), ())), preferred_element_type=F32)


def _norm_proj_kernel(x_ref, nw_ref, w_ref, head_ref, tail_ref, *, chunk):
    xn = _rms(x_ref[...], nw_ref[...]).astype(BF16)
    n_head = head_ref.shape[1]
    for c in range(0, n_head, chunk):
        head_ref[:, c:c + chunk] = _dot(xn, w_ref[:, c:c + chunk])
    for c in range(0, tail_ref.shape[1], chunk):
        tail_ref[:, c:c + chunk] = _dot(xn, w_ref[:, n_head + c:n_head + c + chunk]).astype(BF16)


def norm_proj(x, nw, w, *, layer, tm, n_f32, chunk=512):
    t, d = x.shape
    n = w.shape[2]
    row = lambda i: (i, 0)
    return pl.pallas_call(
        functools.partial(_norm_proj_kernel, chunk=chunk),
        grid=(t // tm,),
        in_specs=[pl.BlockSpec((tm, d), row), _resident((1, d)), _layer(w, layer)],
        out_specs=[pl.BlockSpec((tm, n_f32), row), pl.BlockSpec((tm, n - n_f32), row)],
        out_shape=[jax.ShapeDtypeStruct((t, n_f32), F32), jax.ShapeDtypeStruct((t, n - n_f32), BF16)],
        compiler_params=_cparams(1),
        name="norm_proj",
    )(x, nw, w)


def _ffn_kernel(*refs, chunk, final_norm, has_mixer_out):
    if has_mixer_out:
        y_ref, wo_ref, x_ref, nw_ref, w1_ref, w2_ref, fw_ref, o_ref, h_scr = refs
        x = x_ref[...] + _dot(y_ref[...], wo_ref[...])
    else:
        x_ref, nw_ref, w1_ref, w2_ref, fw_ref, o_ref, h_scr = refs
        x = x_ref[...]
    xn = _rms(x, nw_ref[...]).astype(BF16)
    for c in range(0, h_scr.shape[1], chunk):
        h = jnp.maximum(_dot(xn, w1_ref[:, c:c + chunk]), 0.0)
        h_scr[:, c:c + chunk] = (h * h).astype(BF16)
    y = x + _dot(h_scr[...], w2_ref[...])
    if final_norm:
        y = _rms(y, fw_ref[...])
    o_ref[...] = y


def ffn(x, nw, w1, w2, fw, *, layer, tm, final_norm, mixer_out=None, chunk=1024):
    t, d = x.shape
    dff = w1.shape[2]
    row = lambda i: (i, 0)
    args, specs = [], []
    if mixer_out is not None:
        y, wo, wo_layer = mixer_out
        args += [y, wo]
        specs += [pl.BlockSpec((tm, y.shape[1]), row), _layer(wo, wo_layer)]
    args += [x, nw, w1, w2, fw]
    specs += [pl.BlockSpec((tm, d), row), _resident((1, d)), _layer(w1, layer), _layer(w2, layer),
              _resident((1, d))]
    return pl.pallas_call(
        functools.partial(_ffn_kernel, chunk=chunk, final_norm=final_norm,
                          has_mixer_out=mixer_out is not None),
        grid=(t // tm,),
        in_specs=specs,
        out_specs=pl.BlockSpec((tm, d), row),
        out_shape=jax.ShapeDtypeStruct((t, d), F32),
        scratch_shapes=[pltpu.VMEM((tm, dff), BF16)],
        compiler_params=_cparams(1),
        name="ffn",
    )(*args)


def _retention_kernel(x_ref, nw_ref, w_ref, qg_ref, kg_ref, cos_ref, sin_ref,
                      dm_ref, qd_ref, kd_ref, cd_ref, gw_ref, gb_ref, z_ref, state):
    @pl.when(pl.program_id(1) == 0)
    def _():
        state[...] = jnp.zeros_like(state)

    heads, dk, dv = state.shape
    d = heads * dk
    half = dk // 2
    xn = _rms(x_ref[0], nw_ref[...]).astype(BF16)
    cos = cos_ref[...]
    sin = sin_ref[...]

    def normed_rotated(col, gain_ref, scale):
        y = _rms(_dot(xn, w_ref[:, col:col + dk]), gain_ref[...])
        t1 = y[:, :half]
        t2 = y[:, half:]
        rot = jnp.concatenate([t1 * cos - t2 * sin, t1 * sin + t2 * cos], axis=1)
        return (rot * scale).astype(BF16)

    for h in range(heads):
        q = normed_rotated(h * dk, qg_ref, 1.0)
        k = normed_rotated(d + h * dk, kg_ref, dk ** -0.5)
        v = _dot(xn, w_ref[:, 2 * d + h * dv:2 * d + (h + 1) * dv]).astype(BF16)
        gate = _dot(xn, w_ref[:, 4 * d + h * dv:4 * d + (h + 1) * dv])

        st = state[h]
        p = (_dot_nt(q, k) * dm_ref[h]).astype(BF16)
        y = _dot(p, v) + qd_ref[h] * _dot(q, st.astype(BF16))
        kk = (k.astype(F32) * kd_ref[h]).astype(BF16)
        state[h] = st * cd_ref[h] + _dot_tn(kk, v)

        mu = jnp.mean(y, axis=-1, keepdims=True)
        yc = y - mu
        var = jnp.mean(yc * yc, axis=-1, keepdims=True)
        yn = (yc * lax.rsqrt(var + EPS) * gw_ref[:, h * dv:(h + 1) * dv]
              + gb_ref[:, h * dv:(h + 1) * dv])
        z_ref[0, :, h * dv:(h + 1) * dv] = (gate * jax.nn.sigmoid(gate) * yn).astype(BF16)


def _ret_decay_tables(blk):
    h = jnp.arange(RET_HEADS, dtype=F32)
    log_g = jnp.log(1.0 - jnp.exp2(-5.0 - h))
    idx = jnp.arange(blk, dtype=F32)
    dist = idx[:, None] - idx[None, :]
    ct = jnp.arange(blk)[:, None] // CHUNK
    cs = jnp.arange(blk)[None, :] // CHUNK
    expo = jnp.where(ct == cs, jnp.abs(dist), dist)
    dm = jnp.where((cs <= ct)[None], jnp.exp(log_g[:, None, None] * expo[None]), 0.0)
    qd = jnp.exp(log_g[:, None] * (idx + 1.0))[..., None]
    kd = jnp.exp(log_g[:, None] * (blk - 1.0 - idx))[..., None]
    cd = jnp.exp(log_g * blk)[:, None, None]
    return dm, qd, kd, cd


def retention(x, nw, w, qg, kg, cos, sin, gn_w, gn_b, *, layer, blk):
    b, s, d = x.shape
    dk = d // RET_HEADS
    dv = 2 * d // RET_HEADS
    dm, qd, kd, cd = _ret_decay_tables(blk)
    tok = lambda bi, n: (bi, n, 0)
    pos = lambda bi, n: (n, 0)
    return pl.pallas_call(
        _retention_kernel,
        grid=(b, s // blk),
        in_specs=[pl.BlockSpec((1, blk, d), tok), _resident((1, d)), _layer(w, layer),
                  _resident(qg.shape), _resident(kg.shape),
                  pl.BlockSpec((blk, cos.shape[1]), pos), pl.BlockSpec((blk, sin.shape[1]), pos),
                  _resident(dm.shape), _resident(qd.shape), _resident(kd.shape), _resident(cd.shape),
                  _resident((1, 2 * d)), _resident((1, 2 * d))],
        out_specs=pl.BlockSpec((1, blk, 2 * d), tok),
        out_shape=jax.ShapeDtypeStruct((b, s, 2 * d), BF16),
        scratch_shapes=[pltpu.VMEM((RET_HEADS, dk, dv), F32)],
        compiler_params=_cparams(2),
        name="retention",
    )(x, nw, w, qg, kg, cos, sin, dm, qd, kd, cd, gn_w, gn_b)


def _conv_mixer_kernel(x_ref, nw_ref, w1_ref, b1_ref, dw_ref, dwb_ref, lw_ref, lb_ref, w2_ref, b2_ref,
                       o_ref, pad_scr, conv_scr, taps_scr, *, rows, chunk):
    tc = x_ref.shape[1]
    width = dw_ref.shape[0]
    d = conv_scr.shape[1]

    @pl.when(pl.program_id(1) == 0)
    def _():
        pad_scr[0, 0:CONV_HALO, :] = jnp.zeros((CONV_HALO, d), F32)

    @pl.when(pl.program_id(1) > 0)
    def _():
        pad_scr[0, 0:CONV_HALO, :] = pad_scr[0, tc:tc + CONV_HALO, :]

    @pl.when((pl.program_id(0) == 0) & (pl.program_id(1) == 0))
    def _():
        for j in range(width):
            taps_scr[j] = jnp.broadcast_to(dw_ref[j:j + 1, :], taps_scr.shape[1:])

    x = x_ref[0]
    xn = _rms(x, nw_ref[...]).astype(BF16)
    for c in range(0, d, chunk):
        a = _dot(xn, w1_ref[:, c:c + chunk]) + b1_ref[:, c:c + chunk]
        gate = _dot(xn, w1_ref[:, d + c:d + c + chunk]) + b1_ref[:, d + c:d + c + chunk]
        pad_scr[0, CONV_HALO:CONV_HALO + tc, c:c + chunk] = a * jax.nn.sigmoid(gate)
    span = pad_scr.shape[1] - SUBLANES
    for r in range(1, SUBLANES):
        pad_scr[r, 0:span, :] = pad_scr[0, r:r + span, :]
    first = CONV_HALO - (width - 1)
    for r0 in range(0, tc, rows):
        acc = None
        for j in range(width):
            shift, base = (first + j) % SUBLANES, (first + j) // SUBLANES * SUBLANES
            window = pad_scr[shift, r0 + base:r0 + base + rows, :]
            term = taps_scr[j][None] * window.reshape(rows // SUBLANES, SUBLANES, d)
            acc = term if acc is None else acc + term
        conv_scr[r0:r0 + rows, :] = acc.reshape(rows, d)
    hc = conv_scr[...] + dwb_ref[...]
    mu = jnp.mean(hc, axis=-1, keepdims=True)
    cen = hc - mu
    var = jnp.mean(cen * cen, axis=-1, keepdims=True)
    hn = cen * lax.rsqrt(var + EPS) * lw_ref[...] + lb_ref[...]
    act = (hn * jax.nn.sigmoid(hn)).astype(BF16)
    o_ref[0] = x + _dot(act, w2_ref[...]) + b2_ref[...]


def conv_mixer(x, nw, w1, b1, dw_w, dw_b, ln_w, ln_b, w2, b2, *, layer, tc, rows=16, chunk=512):
    b, s, d = x.shape
    tok = lambda bi, n: (bi, n, 0)
    return pl.pallas_call(
        functools.partial(_conv_mixer_kernel, rows=rows, chunk=chunk),
        grid=(b, s // tc),
        in_specs=[pl.BlockSpec((1, tc, d), tok), _resident((1, d)),
                  _layer(w1, layer), _resident((1, 2 * d)),
                  _resident(dw_w.shape), _resident((1, d)), _resident((1, d)), _resident((1, d)),
                  _layer(w2, layer), _resident((1, d))],
        out_specs=pl.BlockSpec((1, tc, d), tok),
        out_shape=jax.ShapeDtypeStruct((b, s, d), F32),
        scratch_shapes=[pltpu.VMEM((SUBLANES, CONV_HALO + tc, d), F32), pltpu.VMEM((tc, d), F32),
                        pltpu.VMEM((dw_w.shape[0], SUBLANES, d), F32)],
        compiler_params=_cparams(2),
        name="conv_mixer",
    )(x, nw, w1, b1, dw_w, dw_b, ln_w, ln_b, w2, b2)


LOG2E = 1.4426950408889634
SOFTPLUS_CLAMP = 96.0
EXP2_UNDERFLOW = 160.0
BF16_SLACK = 1.0625


def _sb_block(q, kb, u2, carry, causal):
    tk = kb.shape[0]

    def mask_own_rows(t):
        own = jnp.where(causal, t[:tk], 0.0)
        return own if t.shape[0] == tk else jnp.concatenate([own, t[tk:]], axis=0)

    z = _dot_nt(q, kb)
    sp = jnp.maximum(jnp.log(1.0 + jnp.exp2(jnp.minimum(z, SOFTPLUS_CLAMP))) * LOG2E, z)
    if causal is not None:
        sp = mask_own_rows(sp)
    r = _dot(sp.astype(BF16), u2)
    a = jnp.exp2(z - r - jnp.concatenate([carry] * (tk // carry.shape[1]), axis=1))
    if causal is not None:
        a = mask_own_rows(a)
    return a.astype(BF16), jnp.broadcast_to(r[:, 0:1], carry.shape)


def _sb_attn_kernel(q_ref, k_ref, v_ref, qg_ref, kg_ref, u2_ref, o_ref,
                    kn_scr, vn_scr, qn_scr, acc_scr, car_scr, *, dh, tk):
    i = pl.program_id(2)
    tq = q_ref.shape[1]
    nsub = tq // tk
    pair = 128 // dh

    def head_lanes(t, e):
        lane = lax.broadcasted_iota(jnp.int32, t.shape, 1)
        return jnp.where(lane // dh == e, t, 0.0)

    def head_rms(t, e, gain):
        te = head_lanes(t, e)
        ms = jnp.sum(te * te, axis=-1, keepdims=True) * (1.0 / dh)
        return te * lax.rsqrt(ms + EPS) * gain

    @pl.when(i == 0)
    def _():
        for e in range(pair):
            kn_scr[e] = head_rms(k_ref[0], e, kg_ref[...]).astype(BF16)
            vn_scr[e] = head_lanes(v_ref[0].astype(F32), e).astype(BF16)

    for e in range(pair):
        qn_scr[e] = (head_rms(q_ref[0], e, qg_ref[...]) * (LOG2E * dh ** -0.5)).astype(BF16)
    acc_scr[...] = jnp.zeros_like(acc_scr)
    car_scr[...] = jnp.zeros_like(car_scr)

    u2 = u2_ref[...]
    row = lax.broadcasted_iota(jnp.int32, (tk, tk), 0)
    col = lax.broadcasted_iota(jnp.int32, (tk, tk), 1)
    causal = col < row

    def step(block, r0, r1, mask):
        start = pl.multiple_of(block * tk, tk)
        vv = jnp.concatenate([vn_scr[e, pl.ds(start, tk), :] for e in range(pair)], axis=0)
        weights = []
        for e in range(pair):
            a, total = _sb_block(qn_scr[e, r0:r1, :], kn_scr[e, pl.ds(start, tk), :], u2,
                                 car_scr[e, r0:r1, :], mask)
            car_scr[e, r0:r1, :] += total
            weights.append(a)
        acc_scr[r0:r1, :] += _dot(jnp.concatenate(weights, axis=1), vv)

    for c in reversed(range(nsub)):
        step(i * nsub + c, c * tk, min((c + 2) * tk, tq), causal)

    z_bound = (LOG2E * dh ** 0.5 * BF16_SLACK) * jnp.max(jnp.abs(qg_ref[...])) \
        * jnp.max(jnp.abs(kg_ref[...]))
    dead = z_bound + EXP2_UNDERFLOW

    for sub in range(nsub):
        r0, r1 = sub * tk, (sub + 1) * tk
        min_carry = lambda: jnp.min(car_scr[:, r0:r1, :])

        def alive(state):
            block, low = state
            return jnp.logical_and(block >= 0, low <= dead)

        def body(state):
            block, _ = state
            step(block, r0, r1, None)
            return block - 1, min_carry()

        lax.while_loop(alive, body, (i * nsub + sub - (2 if sub else 1), min_carry()))
    o_ref[0] = acc_scr[...].astype(o_ref.dtype)


def sb_attention(qk, v, qg, kg, *, tq, tk):
    b, s, d = v.shape
    dh = d // SB_HEADS
    pair = 128 // dh
    lane_blocks = d // 128
    idx = jnp.arange(tk)
    u2 = (idx[:, None] >= idx[None, :]).astype(BF16)
    return pl.pallas_call(
        functools.partial(_sb_attn_kernel, dh=dh, tk=tk),
        grid=(b, lane_blocks, s // tq),
        in_specs=[pl.BlockSpec((1, tq, 128), lambda bi, hp, i: (bi, i, hp)),
                  pl.BlockSpec((1, s, 128), lambda bi, hp, i: (bi, 0, lane_blocks + hp)),
                  pl.BlockSpec((1, s, 128), lambda bi, hp, i: (bi, 0, hp)),
                  _resident((1, 128)), _resident((1, 128)), _resident((tk, tk))],
        out_specs=pl.BlockSpec((1, tq, 128), lambda bi, hp, i: (bi, i, hp)),
        out_shape=jax.ShapeDtypeStruct((b, s, d), BF16),
        scratch_shapes=[pltpu.VMEM((pair, s, 128), BF16), pltpu.VMEM((pair, s, 128), BF16),
                        pltpu.VMEM((pair, tq, 128), BF16), pltpu.VMEM((tq, 128), F32),
                        pltpu.VMEM((pair, tq, 128), F32)],
        compiler_params=_cparams(3),
        name="sb_attn",
    )(qk, qk, v, jnp.tile(qg, (1, pair)), jnp.tile(kg, (1, pair)), u2)


def _rope_tables(seq, dk):
    half = dk // 2
    inv_freq = ROPE_BASE ** (-jnp.arange(half, dtype=F32) / half)
    ang = jnp.arange(seq, dtype=F32)[:, None] * inv_freq[None, :]
    return jnp.cos(ang), jnp.sin(ang)


def kernel(x, norm_mix, norm_ffn, ret_w_in, ret_q_norm, ret_k_norm, ret_gn_w, ret_gn_b, ret_w_out,
           conv_pw1_w, conv_pw1_b, conv_dw_w, conv_dw_b, conv_ln_w, conv_ln_b, conv_pw2_w, conv_pw2_b,
           sb_w_in, sb_q_norm, sb_k_norm, sb_w_out, ffn_w1, ffn_w2, final_norm):
    b, s, d = x.shape
    depth = norm_mix.shape[0]
    t = b * s
    tm = min(512, s)
    cos, sin = _rope_tables(s, d // RET_HEADS)
    row = lambda a: a.reshape(1, -1)
    ret_w_in, ret_w_out, conv_pw1_w, conv_pw2_w, sb_w_in, sb_w_out, ffn_w1, ffn_w2 = (
        w.astype(BF16) for w in (ret_w_in, ret_w_out, conv_pw1_w, conv_pw2_w, sb_w_in, sb_w_out,
                                 ffn_w1, ffn_w2))

    x2 = x.reshape(t, d)
    for i in range(depth):
        kind = i % N_MIXERS
        j = i // N_MIXERS
        nw = row(norm_mix[i])
        mixer_out = None
        if kind == 0:
            z = retention(x2.reshape(b, s, d), nw, ret_w_in, row(ret_q_norm[j]), row(ret_k_norm[j]),
                          cos, sin, row(ret_gn_w[j]), row(ret_gn_b[j]), layer=j, blk=min(256, s))
            mixer_out = (z.reshape(t, 2 * d), ret_w_out, j)
        elif kind == 1:
            x2 = conv_mixer(x2.reshape(b, s, d), nw, conv_pw1_w, row(conv_pw1_b[j]), conv_dw_w[j],
                            row(conv_dw_b[j]), row(conv_ln_w[j]), row(conv_ln_b[j]), conv_pw2_w,
                            row(conv_pw2_b[j]), layer=j, tc=min(512, s)).reshape(t, d)
        else:
            qk, v = norm_proj(x2, nw, sb_w_in, layer=j, tm=tm, n_f32=2 * d)
            y = sb_attention(qk.reshape(b, s, 2 * d), v.reshape(b, s, d), row(sb_q_norm[j]),
                             row(sb_k_norm[j]), tq=min(1024, s), tk=min(256, s))
            mixer_out = (y.reshape(t, d), sb_w_out, j)
        x2 = ffn(x2, row(norm_ffn[i]), ffn_w1, ffn_w2, row(final_norm), layer=i, tm=tm,
                 final_norm=(i == depth - 1), mixer_out=mixer_out)
    return x2.reshape(b, s, d)
```

```python
import functools

import jax
import jax.numpy as jnp
from jax import lax
from jax.experimental import pallas as pl
from jax.experimental.pallas import tpu as pltpu

F32 = jnp.float32
BF16 = jnp.bfloat16
EPS = 1e-6

CHUNK = 64
RET_HEADS = 4
ROPE_BASE = 10000.0
CONV_WIDTH = 31
SB_HEADS = 16
N_MIXERS = 3

V7X_VMEM_LIMIT_BYTES = 56 * 1024 * 1024
SUBLANES = 8
CONV_HALO = 32


def _cparams(n_axes):
    return pltpu.CompilerParams(
        dimension_semantics=("arbitrary",) * n_axes,
        vmem_limit_bytes=V7X_VMEM_LIMIT_BYTES)


def _resident(shape):
    zeros = (0,) * len(shape)
    return pl.BlockSpec(shape, lambda *_: zeros, pipeline_mode=pl.Buffered(1))


def _side_cast_specs(side, steps, step_of):
    in_specs, out_specs, out_shapes = [], [], []
    for stack, layer in side:
        _, r, c = stack.shape
        in_specs.append(pl.BlockSpec((None, r // steps, c),
                                     lambda *g, layer=layer: (layer, step_of(*g), 0)))
        out_specs.append(pl.BlockSpec((r // steps, c), lambda *g: (step_of(*g), 0)))
        out_shapes.append(jax.ShapeDtypeStruct((r, c), BF16))
    return in_specs, out_specs, out_shapes


def _side_cast(src_refs, dst_refs):
    for src, dst in zip(src_refs, dst_refs):
        dst[...] = src[...].astype(BF16)


def _rms(x, g):
    return x * lax.rsqrt(jnp.mean(x * x, axis=-1, keepdims=True) + EPS) * g


def _dot(a, b):
    return jnp.dot(a, b, preferred_element_type=F32)


def _dot_nt(a, b):
    return lax.dot_general(a, b, (((1,), (1,)), ((), ())), preferred_element_type=F32)


def _dot_tn(a, b):
    return lax.dot_general(a, b, (((0,), (0,)), ((), ())), preferred_element_type=F32)


def _norm_proj_kernel(x_ref, nw_ref, w_ref, head_ref, tail_ref, *, chunk):
    xn = _rms(x_ref[...], nw_ref[...]).astype(BF16)
    n_head = head_ref.shape[1]
    for c in range(0, n_head, chunk):
        head_ref[:, c:c + chunk] = _dot(xn, w_ref[:, c:c + chunk])
    for c in range(0, tail_ref.shape[1], chunk):
        tail_ref[:, c:c + chunk] = _dot(xn, w_ref[:, n_head + c:n_head + c + chunk]).astype(BF16)


def norm_proj(x, nw, w, *, tm, n_f32, chunk=512):
    t, d = x.shape
    n = w.shape[1]
    row = lambda i: (i, 0)
    return pl.pallas_call(
        functools.partial(_norm_proj_kernel, chunk=chunk),
        grid=(t // tm,),
        in_specs=[pl.BlockSpec((tm, d), row), _resident((1, d)), _resident(w.shape)],
        out_specs=[pl.BlockSpec((tm, n_f32), row), pl.BlockSpec((tm, n - n_f32), row)],
        out_shape=[jax.ShapeDtypeStruct((t, n_f32), F32), jax.ShapeDtypeStruct((t, n - n_f32), BF16)],
        compiler_params=_cparams(1),
        name="norm_proj",
    )(x, nw, w)


def _ffn_kernel(*refs, chunk, final_norm, has_mixer_out, n_side):
    refs = list(refs)
    h_scr = refs.pop()
    side_out = [refs.pop() for _ in range(n_side)][::-1]
    o_ref = refs.pop()
    side_in = [refs.pop() for _ in range(n_side)][::-1]
    _side_cast(side_in, side_out)
    if has_mixer_out:
        y_ref, wo_ref, x_ref, nw_ref, w1_ref, w2_ref, fw_ref = refs
        x = x_ref[...] + _dot(y_ref[...], wo_ref[...])
    else:
        x_ref, nw_ref, w1_ref, w2_ref, fw_ref = refs
        x = x_ref[...]
    xn = _rms(x, nw_ref[...]).astype(BF16)
    for c in range(0, h_scr.shape[1], chunk):
        h = jnp.maximum(_dot(xn, w1_ref[:, c:c + chunk]), 0.0)
        h_scr[:, c:c + chunk] = (h * h).astype(BF16)
    y = x + _dot(h_scr[...], w2_ref[...])
    if final_norm:
        y = _rms(y, fw_ref[...])
    o_ref[...] = y


def ffn(x, nw, w1, w2, fw, *, tm, final_norm, mixer_out=None, side=(), chunk=1024):
    t, d = x.shape
    dff = w1.shape[1]
    row = lambda i: (i, 0)
    args, specs = [], []
    if mixer_out is not None:
        y, wo = mixer_out
        args += [y, wo]
        specs += [pl.BlockSpec((tm, y.shape[1]), row), _resident(wo.shape)]
    args += [x, nw, w1, w2, fw]
    specs += [pl.BlockSpec((tm, d), row), _resident((1, d)), _resident(w1.shape),
              _resident(w2.shape), _resident((1, d))]
    side_in, side_out, side_shapes = _side_cast_specs(side, t // tm, lambda i: i)
    out = pl.pallas_call(
        functools.partial(_ffn_kernel, chunk=chunk, final_norm=final_norm,
                          has_mixer_out=mixer_out is not None, n_side=len(side)),
        grid=(t // tm,),
        in_specs=specs + side_in,
        out_specs=[pl.BlockSpec((tm, d), row)] + side_out,
        out_shape=[jax.ShapeDtypeStruct((t, d), F32)] + side_shapes,
        scratch_shapes=[pltpu.VMEM((tm, dff), BF16)],
        compiler_params=_cparams(1),
        name="ffn",
    )(*args, *[stack for stack, _ in side])
    return out[0], out[1:]


def _retention_kernel(x_ref, nw_ref, w_ref, qg_ref, kg_ref, cos_ref, sin_ref,
                      dm_ref, qd_ref, kd_ref, cd_ref, gw_ref, gb_ref, *rest):
    n_side = (len(rest) - 2) // 2
    z_ref, state = rest[n_side], rest[-1]
    _side_cast(rest[:n_side], rest[n_side + 1:-1])

    @pl.when(pl.program_id(1) == 0)
    def _():
        state[...] = jnp.zeros_like(state)

    heads, dk, dv = state.shape
    d = heads * dk
    half = dk // 2
    xn = _rms(x_ref[0], nw_ref[...]).astype(BF16)
    cos = cos_ref[...]
    sin = sin_ref[...]

    def normed_rotated(col, gain_ref, scale):
        y = _rms(_dot(xn, w_ref[:, col:col + dk]), gain_ref[...])
        t1 = y[:, :half]
        t2 = y[:, half:]
        rot = jnp.concatenate([t1 * cos - t2 * sin, t1 * sin + t2 * cos], axis=1)
        return (rot * scale).astype(BF16)

    for h in range(heads):
        q = normed_rotated(h * dk, qg_ref, 1.0)
        k = normed_rotated(d + h * dk, kg_ref, dk ** -0.5)
        v = _dot(xn, w_ref[:, 2 * d + h * dv:2 * d + (h + 1) * dv]).astype(BF16)
        gate = _dot(xn, w_ref[:, 4 * d + h * dv:4 * d + (h + 1) * dv])

        st = state[h]
        p = (_dot_nt(q, k) * dm_ref[h]).astype(BF16)
        y = _dot(p, v) + qd_ref[h] * _dot(q, st.astype(BF16))
        kk = (k.astype(F32) * kd_ref[h]).astype(BF16)
        state[h] = st * cd_ref[h] + _dot_tn(kk, v)

        mu = jnp.mean(y, axis=-1, keepdims=True)
        yc = y - mu
        var = jnp.mean(yc * yc, axis=-1, keepdims=True)
        yn = (yc * lax.rsqrt(var + EPS) * gw_ref[:, h * dv:(h + 1) * dv]
              + gb_ref[:, h * dv:(h + 1) * dv])
        z_ref[0, :, h * dv:(h + 1) * dv] = (gate * jax.nn.sigmoid(gate) * yn).astype(BF16)


def _ret_decay_tables(blk):
    h = jnp.arange(RET_HEADS, dtype=F32)
    log_g = jnp.log(1.0 - jnp.exp2(-5.0 - h))
    idx = jnp.arange(blk, dtype=F32)
    dist = idx[:, None] - idx[None, :]
    ct = jnp.arange(blk)[:, None] // CHUNK
    cs = jnp.arange(blk)[None, :] // CHUNK
    expo = jnp.where(ct == cs, jnp.abs(dist), dist)
    dm = jnp.where((cs <= ct)[None], jnp.exp(log_g[:, None, None] * expo[None]), 0.0)
    qd = jnp.exp(log_g[:, None] * (idx + 1.0))[..., None]
    kd = jnp.exp(log_g[:, None] * (blk - 1.0 - idx))[..., None]
    cd = jnp.exp(log_g * blk)[:, None, None]
    return dm, qd, kd, cd


def retention(x, nw, w, qg, kg, cos, sin, gn_w, gn_b, *, blk, side=()):
    b, s, d = x.shape
    dk = d // RET_HEADS
    dv = 2 * d // RET_HEADS
    nblk = s // blk
    dm, qd, kd, cd = _ret_decay_tables(blk)
    tok = lambda bi, n: (bi, n, 0)
    pos = lambda bi, n: (n, 0)
    side_in, side_out, side_shapes = _side_cast_specs(side, b * nblk, lambda bi, n: bi * nblk + n)
    out = pl.pallas_call(
        _retention_kernel,
        grid=(b, nblk),
        in_specs=[pl.BlockSpec((1, blk, d), tok), _resident((1, d)), _resident(w.shape),
                  _resident(qg.shape), _resident(kg.shape),
                  pl.BlockSpec((blk, cos.shape[1]), pos), pl.BlockSpec((blk, sin.shape[1]), pos),
                  _resident(dm.shape), _resident(qd.shape), _resident(kd.shape), _resident(cd.shape),
                  _resident((1, 2 * d)), _resident((1, 2 * d))] + side_in,
        out_specs=[pl.BlockSpec((1, blk, 2 * d), tok)] + side_out,
        out_shape=[jax.ShapeDtypeStruct((b, s, 2 * d), BF16)] + side_shapes,
        scratch_shapes=[pltpu.VMEM((RET_HEADS, dk, dv), F32)],
        compiler_params=_cparams(2),
        name="retention",
    )(x, nw, w, qg, kg, cos, sin, dm, qd, kd, cd, gn_w, gn_b, *[stack for stack, _ in side])
    return out[0], out[1:]


def _conv_mixer_kernel(x_ref, nw_ref, w1_ref, b1_ref, dw_ref, dwb_ref, lw_ref, lb_ref, w2_ref, b2_ref,
                       o_ref, pad_scr, conv_scr, taps_scr, *, rows, chunk):
    tc = x_ref.shape[1]
    width = dw_ref.shape[0]
    d = conv_scr.shape[1]

    @pl.when(pl.program_id(1) == 0)
    def _():
        pad_scr[0, 0:CONV_HALO, :] = jnp.zeros((CONV_HALO, d), F32)

    @pl.when(pl.program_id(1) > 0)
    def _():
        pad_scr[0, 0:CONV_HALO, :] = pad_scr[0, tc:tc + CONV_HALO, :]

    @pl.when((pl.program_id(0) == 0) & (pl.program_id(1) == 0))
    def _():
        for j in range(width):
            taps_scr[j] = jnp.broadcast_to(dw_ref[j:j + 1, :], taps_scr.shape[1:])

    x = x_ref[0]
    xn = _rms(x, nw_ref[...]).astype(BF16)
    for c in range(0, d, chunk):
        a = _dot(xn, w1_ref[:, c:c + chunk]) + b1_ref[:, c:c + chunk]
        gate = _dot(xn, w1_ref[:, d + c:d + c + chunk]) + b1_ref[:, d + c:d + c + chunk]
        pad_scr[0, CONV_HALO:CONV_HALO + tc, c:c + chunk] = a * jax.nn.sigmoid(gate)
    span = pad_scr.shape[1] - SUBLANES
    for r in range(1, SUBLANES):
        pad_scr[r, 0:span, :] = pad_scr[0, r:r + span, :]
    first = CONV_HALO - (width - 1)
    for r0 in range(0, tc, rows):
        acc = None
        for j in range(width):
            shift, base = (first + j) % SUBLANES, (first + j) // SUBLANES * SUBLANES
            window = pad_scr[shift, r0 + base:r0 + base + rows, :]
            term = taps_scr[j][None] * window.reshape(rows // SUBLANES, SUBLANES, d)
            acc = term if acc is None else acc + term
        conv_scr[r0:r0 + rows, :] = acc.reshape(rows, d)
    hc = conv_scr[...] + dwb_ref[...]
    mu = jnp.mean(hc, axis=-1, keepdims=True)
    cen = hc - mu
    var = jnp.mean(cen * cen, axis=-1, keepdims=True)
    hn = cen * lax.rsqrt(var + EPS) * lw_ref[...] + lb_ref[...]
    act = (hn * jax.nn.sigmoid(hn)).astype(BF16)
    o_ref[0] = x + _dot(act, w2_ref[...]) + b2_ref[...]


def conv_mixer(x, nw, w1, b1, dw_w, dw_b, ln_w, ln_b, w2, b2, *, tc, rows=16, chunk=512):
    b, s, d = x.shape
    tok = lambda bi, n: (bi, n, 0)
    return pl.pallas_call(
        functools.partial(_conv_mixer_kernel, rows=rows, chunk=chunk),
        grid=(b, s // tc),
        in_specs=[pl.BlockSpec((1, tc, d), tok), _resident((1, d)),
                  _resident(w1.shape), _resident((1, 2 * d)),
                  _resident(dw_w.shape), _resident((1, d)), _resident((1, d)), _resident((1, d)),
                  _resident(w2.shape), _resident((1, d))],
        out_specs=pl.BlockSpec((1, tc, d), tok),
        out_shape=jax.ShapeDtypeStruct((b, s, d), F32),
        scratch_shapes=[pltpu.VMEM((SUBLANES, CONV_HALO + tc, d), F32), pltpu.VMEM((tc, d), F32),
                        pltpu.VMEM((dw_w.shape[0], SUBLANES, d), F32)],
        compiler_params=_cparams(2),
        name="conv_mixer",
    )(x, nw, w1, b1, dw_w, dw_b, ln_w, ln_b, w2, b2)


LOG2E = 1.4426950408889634
SOFTPLUS_CLAMP = 96.0
EXP2_UNDERFLOW = 160.0
BF16_SLACK = 1.0625


def _sb_block(q, kb, u2, carry, causal):
    tk = kb.shape[0]

    def mask_own_rows(t):
        own = jnp.where(causal, t[:tk], 0.0)
        return own if t.shape[0] == tk else jnp.concatenate([own, t[tk:]], axis=0)

    z = _dot_nt(q, kb)
    sp = jnp.maximum(jnp.log(1.0 + jnp.exp2(jnp.minimum(z, SOFTPLUS_CLAMP))) * LOG2E, z)
    if causal is not None:
        sp = mask_own_rows(sp)
    r = _dot(sp.astype(BF16), u2)
    a = jnp.exp2(z - r - jnp.concatenate([carry] * (tk // carry.shape[1]), axis=1))
    if causal is not None:
        a = mask_own_rows(a)
    return a.astype(BF16), jnp.broadcast_to(r[:, 0:1], carry.shape)


def _sb_attn_kernel(q_ref, k_ref, v_ref, qg_ref, kg_ref, u2_ref, o_ref,
                    kn_scr, vn_scr, qn_scr, acc_scr, car_scr, *, dh, tk):
    i = pl.program_id(2)
    tq = q_ref.shape[1]
    nsub = tq // tk
    pair = 128 // dh

    def head_lanes(t, e):
        lane = lax.broadcasted_iota(jnp.int32, t.shape, 1)
        return jnp.where(lane // dh == e, t, 0.0)

    def head_rms(t, e, gain):
        te = head_lanes(t, e)
        ms = jnp.sum(te * te, axis=-1, keepdims=True) * (1.0 / dh)
        return te * lax.rsqrt(ms + EPS) * gain

    @pl.when(i == 0)
    def _():
        for e in range(pair):
            kn_scr[e] = head_rms(k_ref[0], e, kg_ref[...]).astype(BF16)
            vn_scr[e] = head_lanes(v_ref[0].astype(F32), e).astype(BF16)

    for e in range(pair):
        qn_scr[e] = (head_rms(q_ref[0], e, qg_ref[...]) * (LOG2E * dh ** -0.5)).astype(BF16)
    acc_scr[...] = jnp.zeros_like(acc_scr)
    car_scr[...] = jnp.zeros_like(car_scr)

    u2 = u2_ref[...]
    row = lax.broadcasted_iota(jnp.int32, (tk, tk), 0)
    col = lax.broadcasted_iota(jnp.int32, (tk, tk), 1)
    causal = col < row

    def step(block, r0, r1, mask):
        start = pl.multiple_of(block * tk, tk)
        vv = jnp.concatenate([vn_scr[e, pl.ds(start, tk), :] for e in range(pair)], axis=0)
        weights = []
        for e in range(pair):
            a, total = _sb_block(qn_scr[e, r0:r1, :], kn_scr[e, pl.ds(start, tk), :], u2,
                                 car_scr[e, r0:r1, :], mask)
            car_scr[e, r0:r1, :] += total
            weights.append(a)
        acc_scr[r0:r1, :] += _dot(jnp.concatenate(weights, axis=1), vv)

    for c in reversed(range(nsub)):
        step(i * nsub + c, c * tk, min((c + 2) * tk, tq), causal)

    z_bound = (LOG2E * dh ** 0.5 * BF16_SLACK) * jnp.max(jnp.abs(qg_ref[...])) \
        * jnp.max(jnp.abs(kg_ref[...]))
    dead = z_bound + EXP2_UNDERFLOW

    def min_carry(sub):
        return jnp.min(car_scr[:, sub * tk:(sub + 1) * tk, :])

    lows = [min_carry(sub) for sub in range(nsub)]
    for sub in range(nsub):
        def alive(state):
            block, low = state
            return jnp.logical_and(block >= 0, low <= dead)

        def body(state):
            block, _ = state
            step(block, sub * tk, (sub + 1) * tk, None)
            return block - 1, min_carry(sub)

        lax.while_loop(alive, body, (i * nsub + sub - (2 if sub else 1), lows[sub]))
    o_ref[0] = acc_scr[...].astype(o_ref.dtype)


def sb_attention(qk, v, qg, kg, *, tq, tk):
    b, s, d = v.shape
    dh = d // SB_HEADS
    pair = 128 // dh
    lane_blocks = d // 128
    idx = jnp.arange(tk)
    u2 = (idx[:, None] >= idx[None, :]).astype(BF16)
    return pl.pallas_call(
        functools.partial(_sb_attn_kernel, dh=dh, tk=tk),
        grid=(b, lane_blocks, s // tq),
        in_specs=[pl.BlockSpec((1, tq, 128), lambda bi, hp, i: (bi, i, hp)),
                  pl.BlockSpec((1, s, 128), lambda bi, hp, i: (bi, 0, lane_blocks + hp)),
                  pl.BlockSpec((1, s, 128), lambda bi, hp, i: (bi, 0, hp)),
                  _resident((1, 128)), _resident((1, 128)), _resident((tk, tk))],
        out_specs=pl.BlockSpec((1, tq, 128), lambda bi, hp, i: (bi, i, hp)),
        out_shape=jax.ShapeDtypeStruct((b, s, d), BF16),
        scratch_shapes=[pltpu.VMEM((pair, s, 128), BF16), pltpu.VMEM((pair, s, 128), BF16),
                        pltpu.VMEM((pair, tq, 128), BF16), pltpu.VMEM((tq, 128), F32),
                        pltpu.VMEM((pair, tq, 128), F32)],
        compiler_params=_cparams(3),
        name="sb_attn",
    )(qk, qk, v, jnp.tile(qg, (1, pair)), jnp.tile(kg, (1, pair)), u2)


def _rope_tables(seq, dk):
    half = dk // 2
    inv_freq = ROPE_BASE ** (-jnp.arange(half, dtype=F32) / half)
    ang = jnp.arange(seq, dtype=F32)[:, None] * inv_freq[None, :]
    return jnp.cos(ang), jnp.sin(ang)


def kernel(x, norm_mix, norm_ffn, ret_w_in, ret_q_norm, ret_k_norm, ret_gn_w, ret_gn_b, ret_w_out,
           conv_pw1_w, conv_pw1_b, conv_dw_w, conv_dw_b, conv_ln_w, conv_ln_b, conv_pw2_w, conv_pw2_b,
           sb_w_in, sb_q_norm, sb_k_norm, sb_w_out, ffn_w1, ffn_w2, final_norm):
    b, s, d = x.shape
    depth = norm_mix.shape[0]
    t = b * s
    tm = min(512, s)
    cos, sin = _rope_tables(s, d // RET_HEADS)
    row = lambda a: a.reshape(1, -1)

    stacks = {"ret_in": ret_w_in, "ret_out": ret_w_out, "conv_in": conv_pw1_w, "conv_out": conv_pw2_w,
              "sb_in": sb_w_in, "sb_out": sb_w_out, "ffn1": ffn_w1, "ffn2": ffn_w2}
    mixer_names = (("ret_in", "ret_out"), ("conv_in", "conv_out"), ("sb_in", "sb_out"))

    def layer_weights(i):
        keys = [(name, i // N_MIXERS) for name in mixer_names[i % N_MIXERS]]
        return keys + [("ffn1", i), ("ffn2", i)]

    def side_job(keys):
        return [(stacks[name], idx) for name, idx in keys]

    first, *rest_of_layer0 = layer_weights(0)
    bf16_w = {first: stacks[first[0]][first[1]].astype(BF16)}

    x2 = x.reshape(t, d)
    for i in range(depth):
        kind = i % N_MIXERS
        j = i // N_MIXERS
        nw = row(norm_mix[i])
        mixer_out = None
        if kind == 0:
            pending = rest_of_layer0 if i == 0 else []
            z, cast = retention(x2.reshape(b, s, d), nw, bf16_w[("ret_in", j)], row(ret_q_norm[j]),
                                row(ret_k_norm[j]), cos, sin, row(ret_gn_w[j]), row(ret_gn_b[j]),
                                blk=min(256, s), side=side_job(pending))
            bf16_w.update(zip(pending, cast))
            mixer_out = (z.reshape(t, 2 * d), bf16_w[("ret_out", j)])
        elif kind == 1:
            x2 = conv_mixer(x2.reshape(b, s, d), nw, bf16_w[("conv_in", j)], row(conv_pw1_b[j]),
                            conv_dw_w[j], row(conv_dw_b[j]), row(conv_ln_w[j]), row(conv_ln_b[j]),
                            bf16_w[("conv_out", j)], row(conv_pw2_b[j]),
                            tc=min(512, s)).reshape(t, d)
        else:
            qk, v = norm_proj(x2, nw, bf16_w[("sb_in", j)], tm=tm, n_f32=2 * d)
            y = sb_attention(qk.reshape(b, s, 2 * d), v.reshape(b, s, d), row(sb_q_norm[j]),
                             row(sb_k_norm[j]), tq=min(1024, s), tk=min(256, s))
            mixer_out = (y.reshape(t, d), bf16_w[("sb_out", j)])
        pending = layer_weights(i + 1) if i + 1 < depth else []
        x2, cast = ffn(x2, row(norm_ffn[i]), bf16_w[("ffn1", i)], bf16_w[("ffn2", i)],
                       row(final_norm), tm=tm, final_norm=(i == depth - 1), mixer_out=mixer_out,
                       side=side_job(pending))
        bf16_w.update(zip(pending, cast))
    return x2.reshape(b, s, d)
```

```python
import functools

import jax
import jax.numpy as jnp
from jax import lax
from jax.experimental import pallas as pl
from jax.experimental.pallas import tpu as pltpu

F32 = jnp.float32
BF16 = jnp.bfloat16
EPS = 1e-6

CHUNK = 64
RET_HEADS = 4
ROPE_BASE = 10000.0
SB_HEADS = 16
N_MIXERS = 3

V7X_VMEM_LIMIT_BYTES = 56 * 1024 * 1024
SUBLANES = 8
BF16_SUBLANES = 16
CONV_HALO = 32


def _cparams(n_axes):
    return pltpu.CompilerParams(
        dimension_semantics=("arbitrary",) * n_axes,
        vmem_limit_bytes=V7X_VMEM_LIMIT_BYTES)


def _resident(shape):
    zeros = (0,) * len(shape)
    return pl.BlockSpec(shape, lambda *_: zeros, pipeline_mode=pl.Buffered(1))


def _side_cast_specs(side, steps, step_of):
    in_specs, out_specs, out_shapes = [], [], []
    for stack, layer in side:
        _, r, c = stack.shape
        tile, rem = divmod(r, steps)
        assert rem == 0 and tile % BF16_SUBLANES == 0, (stack.shape, steps)
        in_specs.append(pl.BlockSpec((None, tile, c),
                                     lambda *g, layer=layer: (layer, step_of(*g), 0)))
        out_specs.append(pl.BlockSpec((tile, c), lambda *g: (step_of(*g), 0)))
        out_shapes.append(jax.ShapeDtypeStruct((r, c), BF16))
    return in_specs, out_specs, out_shapes


def _side_cast(src_refs, dst_refs):
    for src, dst in zip(src_refs, dst_refs):
        dst[...] = src[...].astype(BF16)


def _rms(x, g):
    return x * lax.rsqrt(jnp.mean(x * x, axis=-1, keepdims=True) + EPS) * g


def _dot(a, b):
    return jnp.dot(a, b, preferred_element_type=F32)


def _dot_nt(a, b):
    return lax.dot_general(a, b, (((1,), (1,)), ((), ())), preferred_element_type=F32)


def _dot_tn(a, b):
    return lax.dot_general(a, b, (((0,), (0,)), ((), ())), preferred_element_type=F32)


def _norm_proj_kernel(x_ref, nw_ref, w_ref, head_ref, tail_ref, *, chunk):
    xn = _rms(x_ref[...], nw_ref[...]).astype(BF16)
    n_head = head_ref.shape[1]
    for c in range(0, n_head, chunk):
        head_ref[:, c:c + chunk] = _dot(xn, w_ref[:, c:c + chunk])
    for c in range(0, tail_ref.shape[1], chunk):
        tail_ref[:, c:c + chunk] = _dot(xn, w_ref[:, n_head + c:n_head + c + chunk]).astype(BF16)


def norm_proj(x, nw, w, *, tm, n_f32, chunk=512):
    t, d = x.shape
    n = w.shape[1]
    row = lambda i: (i, 0)
    return pl.pallas_call(
        functools.partial(_norm_proj_kernel, chunk=chunk),
        grid=(t // tm,),
        in_specs=[pl.BlockSpec((tm, d), row), _resident((1, d)), _resident(w.shape)],
        out_specs=[pl.BlockSpec((tm, n_f32), row), pl.BlockSpec((tm, n - n_f32), row)],
        out_shape=[jax.ShapeDtypeStruct((t, n_f32), F32), jax.ShapeDtypeStruct((t, n - n_f32), BF16)],
        compiler_params=_cparams(1),
        name="norm_proj",
    )(x, nw, w)


def _ffn_kernel(*refs, chunk, final_norm, has_mixer_out, n_side):
    refs = list(refs)
    h_scr = refs.pop()
    side_out = [refs.pop() for _ in range(n_side)][::-1]
    o_ref = refs.pop()
    side_in = [refs.pop() for _ in range(n_side)][::-1]
    _side_cast(side_in, side_out)
    if has_mixer_out:
        y_ref, wo_ref, x_ref, nw_ref, w1_ref, w2_ref, fw_ref = refs
        x = x_ref[...] + _dot(y_ref[...], wo_ref[...])
    else:
        x_ref, nw_ref, w1_ref, w2_ref, fw_ref = refs
        x = x_ref[...]
    xn = _rms(x, nw_ref[...]).astype(BF16)
    for c in range(0, h_scr.shape[1], chunk):
        h = jnp.maximum(_dot(xn, w1_ref[:, c:c + chunk]), 0.0)
        h_scr[:, c:c + chunk] = (h * h).astype(BF16)
    y = x + _dot(h_scr[...], w2_ref[...])
    if final_norm:
        y = _rms(y, fw_ref[...])
    o_ref[...] = y


def ffn(x, nw, w1, w2, fw, *, tm, final_norm, mixer_out=None, side=(), chunk=1024):
    t, d = x.shape
    dff = w1.shape[1]
    row = lambda i: (i, 0)
    args, specs = [], []
    if mixer_out is not None:
        y, wo = mixer_out
        args += [y, wo]
        specs += [pl.BlockSpec((tm, y.shape[1]), row), _resident(wo.shape)]
    args += [x, nw, w1, w2, fw]
    specs += [pl.BlockSpec((tm, d), row), _resident((1, d)), _resident(w1.shape),
              _resident(w2.shape), _resident((1, d))]
    side_in, side_out, side_shapes = _side_cast_specs(side, t // tm, lambda i: i)
    out = pl.pallas_call(
        functools.partial(_ffn_kernel, chunk=chunk, final_norm=final_norm,
                          has_mixer_out=mixer_out is not None, n_side=len(side)),
        grid=(t // tm,),
        in_specs=specs + side_in,
        out_specs=[pl.BlockSpec((tm, d), row)] + side_out,
        out_shape=[jax.ShapeDtypeStruct((t, d), F32)] + side_shapes,
        scratch_shapes=[pltpu.VMEM((tm, dff), BF16)],
        compiler_params=_cparams(1),
        name="ffn",
    )(*args, *[stack for stack, _ in side])
    return out[0], out[1:]


def _retention_kernel(x_ref, nw_ref, w_ref, qg_ref, kg_ref, cos_ref, sin_ref,
                      dm_ref, qd_ref, kd_ref, cd_ref, gw_ref, gb_ref, *rest):
    n_side = (len(rest) - 2) // 2
    z_ref, state = rest[n_side], rest[-1]
    _side_cast(rest[:n_side], rest[n_side + 1:-1])

    @pl.when(pl.program_id(1) == 0)
    def _():
        state[...] = jnp.zeros_like(state)

    nseq, heads, dk, dv = state.shape
    d = heads * dk
    half = dk // 2
    blk = x_ref.shape[1]
    xn = _rms(x_ref[...].reshape(nseq * blk, d), nw_ref[...]).astype(BF16)
    cos = jnp.concatenate([cos_ref[...]] * nseq, axis=0)
    sin = jnp.concatenate([sin_ref[...]] * nseq, axis=0)

    def normed_rotated(col, gain_ref, scale):
        y = _rms(_dot(xn, w_ref[:, col:col + dk]), gain_ref[...])
        t1 = y[:, :half]
        t2 = y[:, half:]
        rot = jnp.concatenate([t1 * cos - t2 * sin, t1 * sin + t2 * cos], axis=1)
        return (rot * scale).astype(BF16)

    for h in range(heads):
        q_all = normed_rotated(h * dk, qg_ref, 1.0)
        k_all = normed_rotated(d + h * dk, kg_ref, dk ** -0.5)
        v_all = _dot(xn, w_ref[:, 2 * d + h * dv:2 * d + (h + 1) * dv]).astype(BF16)
        gate_all = _dot(xn, w_ref[:, 4 * d + h * dv:4 * d + (h + 1) * dv])
        for sq in range(nseq):
            rows = slice(sq * blk, (sq + 1) * blk)
            q, k, v, gate = q_all[rows], k_all[rows], v_all[rows], gate_all[rows]
            st = state[sq, h]
            p = (_dot_nt(q, k) * dm_ref[h]).astype(BF16)
            y = _dot(p, v) + qd_ref[h] * _dot(q, st.astype(BF16))
            kk = (k.astype(F32) * kd_ref[h]).astype(BF16)
            state[sq, h] = st * cd_ref[h] + _dot_tn(kk, v)

            mu = jnp.mean(y, axis=-1, keepdims=True)
            yc = y - mu
            var = jnp.mean(yc * yc, axis=-1, keepdims=True)
            yn = (yc * lax.rsqrt(var + EPS) * gw_ref[:, h * dv:(h + 1) * dv]
                  + gb_ref[:, h * dv:(h + 1) * dv])
            z_ref[sq, :, h * dv:(h + 1) * dv] = (gate * jax.nn.sigmoid(gate) * yn).astype(BF16)


def _ret_decay_tables(blk):
    h = jnp.arange(RET_HEADS, dtype=F32)
    log_g = jnp.log(1.0 - jnp.exp2(-5.0 - h))
    idx = jnp.arange(blk, dtype=F32)
    dist = idx[:, None] - idx[None, :]
    ct = jnp.arange(blk)[:, None] // CHUNK
    cs = jnp.arange(blk)[None, :] // CHUNK
    expo = jnp.where(ct == cs, jnp.abs(dist), dist)
    dm = jnp.where((cs <= ct)[None], jnp.exp(log_g[:, None, None] * expo[None]), 0.0)
    qd = jnp.exp(log_g[:, None] * (idx + 1.0))[..., None]
    kd = jnp.exp(log_g[:, None] * (blk - 1.0 - idx))[..., None]
    cd = jnp.exp(log_g * blk)[:, None, None]
    return dm, qd, kd, cd


def retention(x, nw, w, qg, kg, cos, sin, gn_w, gn_b, *, blk, side=()):
    b, s, d = x.shape
    dk = d // RET_HEADS
    dv = 2 * d // RET_HEADS
    nblk = s // blk
    nseq = 2 if b % 2 == 0 else 1
    dm, qd, kd, cd = _ret_decay_tables(blk)
    tok = lambda bi, n: (bi, n, 0)
    pos = lambda bi, n: (n, 0)
    side_in, side_out, side_shapes = _side_cast_specs(side, b // nseq * nblk,
                                                      lambda bi, n: bi * nblk + n)
    out = pl.pallas_call(
        _retention_kernel,
        grid=(b // nseq, nblk),
        in_specs=[pl.BlockSpec((nseq, blk, d), tok), _resident((1, d)), _resident(w.shape),
                  _resident(qg.shape), _resident(kg.shape),
                  pl.BlockSpec((blk, cos.shape[1]), pos), pl.BlockSpec((blk, sin.shape[1]), pos),
                  _resident(dm.shape), _resident(qd.shape), _resident(kd.shape), _resident(cd.shape),
                  _resident((1, 2 * d)), _resident((1, 2 * d))] + side_in,
        out_specs=[pl.BlockSpec((nseq, blk, 2 * d), tok)] + side_out,
        out_shape=[jax.ShapeDtypeStruct((b, s, 2 * d), BF16)] + side_shapes,
        scratch_shapes=[pltpu.VMEM((nseq, RET_HEADS, dk, dv), F32)],
        compiler_params=_cparams(2),
        name="retention",
    )(x, nw, w, qg, kg, cos, sin, dm, qd, kd, cd, gn_w, gn_b, *[stack for stack, _ in side])
    return out[0], out[1:]


def _conv_mixer_kernel(x_ref, nw_ref, w1_ref, b1_ref, dw_ref, dwb_ref, lw_ref, lb_ref, w2_ref, b2_ref,
                       o_ref, pad_scr, conv_scr, taps_scr, *, rows, chunk):
    tc = x_ref.shape[1]
    width = dw_ref.shape[0]
    d = conv_scr.shape[1]

    @pl.when(pl.program_id(1) == 0)
    def _():
        pad_scr[0, 0:CONV_HALO, :] = jnp.zeros((CONV_HALO, d), F32)

    @pl.when(pl.program_id(1) > 0)
    def _():
        pad_scr[0, 0:CONV_HALO, :] = pad_scr[0, tc:tc + CONV_HALO, :]

    @pl.when((pl.program_id(0) == 0) & (pl.program_id(1) == 0))
    def _():
        for j in range(width):
            taps_scr[j] = jnp.broadcast_to(dw_ref[j:j + 1, :], taps_scr.shape[1:])

    x = x_ref[0]
    xn = _rms(x, nw_ref[...]).astype(BF16)
    for c in range(0, d, chunk):
        a = _dot(xn, w1_ref[:, c:c + chunk]) + b1_ref[:, c:c + chunk]
        gate = _dot(xn, w1_ref[:, d + c:d + c + chunk]) + b1_ref[:, d + c:d + c + chunk]
        pad_scr[0, CONV_HALO:CONV_HALO + tc, c:c + chunk] = a * jax.nn.sigmoid(gate)
    span = pad_scr.shape[1] - SUBLANES
    for r in range(1, SUBLANES):
        pad_scr[r, 0:span, :] = pad_scr[0, r:r + span, :]
    first = CONV_HALO - (width - 1)
    for r0 in range(0, tc, rows):
        acc = None
        for j in range(width):
            shift, base = (first + j) % SUBLANES, (first + j) // SUBLANES * SUBLANES
            window = pad_scr[shift, r0 + base:r0 + base + rows, :]
            term = taps_scr[j][None] * window.reshape(rows // SUBLANES, SUBLANES, d)
            acc = term if acc is None else acc + term
        conv_scr[r0:r0 + rows, :] = acc.reshape(rows, d)
    hc = conv_scr[...] + dwb_ref[...]
    mu = jnp.mean(hc, axis=-1, keepdims=True)
    cen = hc - mu
    var = jnp.mean(cen * cen, axis=-1, keepdims=True)
    hn = cen * lax.rsqrt(var + EPS) * lw_ref[...] + lb_ref[...]
    act = (hn * jax.nn.sigmoid(hn)).astype(BF16)
    o_ref[0] = x + _dot(act, w2_ref[...]) + b2_ref[...]


def conv_mixer(x, nw, w1, b1, dw_w, dw_b, ln_w, ln_b, w2, b2, *, tc, rows=16, chunk=512):
    b, s, d = x.shape
    assert dw_w.shape[0] - 1 <= CONV_HALO <= tc and s % tc == 0 and tc % rows == 0
    tok = lambda bi, n: (bi, n, 0)
    return pl.pallas_call(
        functools.partial(_conv_mixer_kernel, rows=rows, chunk=chunk),
        grid=(b, s // tc),
        in_specs=[pl.BlockSpec((1, tc, d), tok), _resident((1, d)),
                  _resident(w1.shape), _resident((1, 2 * d)),
                  _resident(dw_w.shape), _resident((1, d)), _resident((1, d)), _resident((1, d)),
                  _resident(w2.shape), _resident((1, d))],
        out_specs=pl.BlockSpec((1, tc, d), tok),
        out_shape=jax.ShapeDtypeStruct((b, s, d), F32),
        scratch_shapes=[pltpu.VMEM((SUBLANES, CONV_HALO + tc, d), F32), pltpu.VMEM((tc, d), F32),
                        pltpu.VMEM((dw_w.shape[0], SUBLANES, d), F32)],
        compiler_params=_cparams(2),
        name="conv_mixer",
    )(x, nw, w1, b1, dw_w, dw_b, ln_w, ln_b, w2, b2)


LOG2E = 1.4426950408889634
SOFTPLUS_CLAMP = 96.0
EXP2_UNDERFLOW = 160.0
BF16_SLACK = 1.0625


def _sb_block(q, kb, u2, carry, causal):
    tk = kb.shape[0]

    def mask_own_rows(t):
        own = jnp.where(causal, t[:tk], 0.0)
        return own if t.shape[0] == tk else jnp.concatenate([own, t[tk:]], axis=0)

    z = _dot_nt(q, kb)
    sp = jnp.maximum(jnp.log(1.0 + jnp.exp2(jnp.minimum(z, SOFTPLUS_CLAMP))) * LOG2E, z)
    if causal is not None:
        sp = mask_own_rows(sp)
    r = _dot(sp.astype(BF16), u2)
    a = jnp.exp2(z - r - jnp.concatenate([carry] * (tk // carry.shape[1]), axis=1))
    if causal is not None:
        a = mask_own_rows(a)
    return a.astype(BF16), jnp.broadcast_to(r[:, 0:1], carry.shape)


def _sb_attn_kernel(q_ref, k_ref, v_ref, qg_ref, kg_ref, u2_ref, o_ref,
                    kn_scr, vn_scr, qn_scr, acc_scr, car_scr, *, dh, tk):
    i = pl.program_id(2)
    tq = q_ref.shape[1]
    nsub = tq // tk
    pair = 128 // dh

    def head_lanes(t, e):
        lane = lax.broadcasted_iota(jnp.int32, t.shape, 1)
        return jnp.where(lane // dh == e, t, 0.0)

    def head_rms(t, e, gain):
        te = head_lanes(t, e)
        ms = jnp.sum(te * te, axis=-1, keepdims=True) * (1.0 / dh)
        return te * lax.rsqrt(ms + EPS) * gain

    @pl.when(i == 0)
    def _():
        for e in range(pair):
            kn_scr[e] = head_rms(k_ref[0], e, kg_ref[...]).astype(BF16)
            vn_scr[e] = head_lanes(v_ref[0].astype(F32), e).astype(BF16)

    for e in range(pair):
        qn_scr[e] = (head_rms(q_ref[0], e, qg_ref[...]) * (LOG2E * dh ** -0.5)).astype(BF16)
    acc_scr[...] = jnp.zeros_like(acc_scr)
    car_scr[...] = jnp.zeros_like(car_scr)

    u2 = u2_ref[...]
    row = lax.broadcasted_iota(jnp.int32, (tk, tk), 0)
    col = lax.broadcasted_iota(jnp.int32, (tk, tk), 1)
    causal = col < row

    def step(block, r0, r1, mask):
        start = pl.multiple_of(block * tk, tk)
        vv = jnp.concatenate([vn_scr[e, pl.ds(start, tk), :] for e in range(pair)], axis=0)
        weights = []
        for e in range(pair):
            a, total = _sb_block(qn_scr[e, r0:r1, :], kn_scr[e, pl.ds(start, tk), :], u2,
                                 car_scr[e, r0:r1, :], mask)
            car_scr[e, r0:r1, :] += total
            weights.append(a)
        acc_scr[r0:r1, :] += _dot(jnp.concatenate(weights, axis=1), vv)

    for c in reversed(range(nsub)):
        step(i * nsub + c, c * tk, min((c + 2) * tk, tq), causal)

    z_bound = (LOG2E * dh ** 0.5 * BF16_SLACK) * jnp.max(jnp.abs(qg_ref[...])) \
        * jnp.max(jnp.abs(kg_ref[...]))
    dead = z_bound + EXP2_UNDERFLOW

    def min_carry(sub):
        return jnp.min(car_scr[:, sub * tk:(sub + 1) * tk, :])

    lows = [min_carry(sub) for sub in range(nsub)]
    for sub in range(nsub):
        def alive(state):
            block, low = state
            return jnp.logical_and(block >= 0, low <= dead)

        def body(state):
            block, _ = state
            step(block, sub * tk, (sub + 1) * tk, None)
            return block - 1, min_carry(sub)

        lax.while_loop(alive, body, (i * nsub + sub - (2 if sub else 1), lows[sub]))
    o_ref[0] = acc_scr[...].astype(o_ref.dtype)


def sb_attention(qk, v, qg, kg, *, tq, tk):
    b, s, d = v.shape
    dh = d // SB_HEADS
    pair = 128 // dh
    lane_blocks = d // 128
    idx = jnp.arange(tk)
    u2 = (idx[:, None] >= idx[None, :]).astype(BF16)
    return pl.pallas_call(
        functools.partial(_sb_attn_kernel, dh=dh, tk=tk),
        grid=(b, lane_blocks, s // tq),
        in_specs=[pl.BlockSpec((1, tq, 128), lambda bi, hp, i: (bi, i, hp)),
                  pl.BlockSpec((1, s, 128), lambda bi, hp, i: (bi, 0, lane_blocks + hp)),
                  pl.BlockSpec((1, s, 128), lambda bi, hp, i: (bi, 0, hp)),
                  _resident((1, 128)), _resident((1, 128)), _resident((tk, tk))],
        out_specs=pl.BlockSpec((1, tq, 128), lambda bi, hp, i: (bi, i, hp)),
        out_shape=jax.ShapeDtypeStruct((b, s, d), BF16),
        scratch_shapes=[pltpu.VMEM((pair, s, 128), BF16), pltpu.VMEM((pair, s, 128), BF16),
                        pltpu.VMEM((pair, tq, 128), BF16), pltpu.VMEM((tq, 128), F32),
                        pltpu.VMEM((pair, tq, 128), F32)],
        compiler_params=_cparams(3),
        name="sb_attn",
    )(qk, qk, v, jnp.tile(qg, (1, pair)), jnp.tile(kg, (1, pair)), u2)


def _rope_tables(seq, dk):
    half = dk // 2
    inv_freq = ROPE_BASE ** (-jnp.arange(half, dtype=F32) / half)
    ang = jnp.arange(seq, dtype=F32)[:, None] * inv_freq[None, :]
    return jnp.cos(ang), jnp.sin(ang)


def kernel(x, norm_mix, norm_ffn, ret_w_in, ret_q_norm, ret_k_norm, ret_gn_w, ret_gn_b, ret_w_out,
           conv_pw1_w, conv_pw1_b, conv_dw_w, conv_dw_b, conv_ln_w, conv_ln_b, conv_pw2_w, conv_pw2_b,
           sb_w_in, sb_q_norm, sb_k_norm, sb_w_out, ffn_w1, ffn_w2, final_norm):
    b, s, d = x.shape
    depth = norm_mix.shape[0]
    t = b * s
    tm = min(512, s)
    cos, sin = _rope_tables(s, d // RET_HEADS)
    row = lambda a: a.reshape(1, -1)

    stacks = {"ret_in": ret_w_in, "ret_out": ret_w_out, "conv_in": conv_pw1_w, "conv_out": conv_pw2_w,
              "sb_in": sb_w_in, "sb_out": sb_w_out, "ffn1": ffn_w1, "ffn2": ffn_w2}
    mixer_names = (("ret_in", "ret_out"), ("conv_in", "conv_out"), ("sb_in", "sb_out"))

    def layer_weights(i):
        keys = [(name, i // N_MIXERS) for name in mixer_names[i % N_MIXERS]]
        return keys + [("ffn1", i), ("ffn2", i)]

    def side_job(keys):
        return [(stacks[name], idx) for name, idx in keys]

    first, *rest_of_layer0 = layer_weights(0)
    bf16_w = {first: stacks[first[0]][first[1]].astype(BF16)}

    x2 = x.reshape(t, d)
    for i in range(depth):
        kind = i % N_MIXERS
        j = i // N_MIXERS
        nw = row(norm_mix[i])
        mixer_out = None
        if kind == 0:
            pending = rest_of_layer0 if i == 0 else []
            z, cast = retention(x2.reshape(b, s, d), nw, bf16_w[("ret_in", j)], row(ret_q_norm[j]),
                                row(ret_k_norm[j]), cos, sin, row(ret_gn_w[j]), row(ret_gn_b[j]),
                                blk=min(256, s), side=side_job(pending))
            bf16_w.update(zip(pending, cast))
            mixer_out = (z.reshape(t, 2 * d), bf16_w[("ret_out", j)])
        elif kind == 1:
            x2 = conv_mixer(x2.reshape(b, s, d), nw, bf16_w[("conv_in", j)], row(conv_pw1_b[j]),
                            conv_dw_w[j], row(conv_dw_b[j]), row(conv_ln_w[j]), row(conv_ln_b[j]),
                            bf16_w[("conv_out", j)], row(conv_pw2_b[j]),
                            tc=min(512, s)).reshape(t, d)
        else:
            qk, v = norm_proj(x2, nw, bf16_w[("sb_in", j)], tm=tm, n_f32=2 * d)
            y = sb_attention(qk.reshape(b, s, 2 * d), v.reshape(b, s, d), row(sb_q_norm[j]),
                             row(sb_k_norm[j]), tq=min(1024, s), tk=min(256, s))
            mixer_out = (y.reshape(t, d), bf16_w[("sb_out", j)])
        pending = layer_weights(i + 1) if i + 1 < depth else []
        x2, cast = ffn(x2, row(norm_ffn[i]), bf16_w[("ffn1", i)], bf16_w[("ffn2", i)],
                       row(final_norm), tm=tm, final_norm=(i == depth - 1), mixer_out=mixer_out,
                       side=side_job(pending))
        bf16_w.update(zip(pending, cast))
    return x2.reshape(b, s, d)
```

```python
import functools

import jax
import jax.numpy as jnp
from jax import lax
from jax.experimental import pallas as pl
from jax.experimental.pallas import tpu as pltpu

F32 = jnp.float32
BF16 = jnp.bfloat16
EPS = 1e-6

CHUNK = 64
RET_HEADS = 4
ROPE_BASE = 10000.0
SB_HEADS = 16
N_MIXERS = 3

V7X_VMEM_LIMIT_BYTES = 56 * 1024 * 1024
SUBLANES = 8
BF16_SUBLANES = 16
CONV_HALO = 32


def _cparams(n_axes):
    return pltpu.CompilerParams(
        dimension_semantics=("arbitrary",) * n_axes,
        vmem_limit_bytes=V7X_VMEM_LIMIT_BYTES)


def _resident(shape):
    zeros = (0,) * len(shape)
    return pl.BlockSpec(shape, lambda *_: zeros, pipeline_mode=pl.Buffered(1))


def _side_cast_specs(side, steps, step_of):
    in_specs, out_specs, out_shapes = [], [], []
    for stack, layer in side:
        _, r, c = stack.shape
        tile, rem = divmod(r, steps)
        assert rem == 0 and tile % BF16_SUBLANES == 0, (stack.shape, steps)
        in_specs.append(pl.BlockSpec((None, tile, c),
                                     lambda *g, layer=layer: (layer, step_of(*g), 0)))
        out_specs.append(pl.BlockSpec((tile, c), lambda *g: (step_of(*g), 0)))
        out_shapes.append(jax.ShapeDtypeStruct((r, c), BF16))
    return in_specs, out_specs, out_shapes


def _side_cast(src_refs, dst_refs):
    for src, dst in zip(src_refs, dst_refs):
        dst[...] = src[...].astype(BF16)


def _rms(x, g):
    return x * lax.rsqrt(jnp.mean(x * x, axis=-1, keepdims=True) + EPS) * g


def _dot(a, b):
    return jnp.dot(a, b, preferred_element_type=F32)


def _dot_nt(a, b):
    return lax.dot_general(a, b, (((1,), (1,)), ((), ())), preferred_element_type=F32)


def _dot_tn(a, b):
    return lax.dot_general(a, b, (((0,), (0,)), ((), ())), preferred_element_type=F32)


def _norm_proj_kernel(x_ref, nw_ref, w_ref, head_ref, tail_ref, *, chunk):
    xn = _rms(x_ref[...], nw_ref[...]).astype(BF16)
    n_head = head_ref.shape[1]
    for c in range(0, n_head, chunk):
        head_ref[:, c:c + chunk] = _dot(xn, w_ref[:, c:c + chunk])
    for c in range(0, tail_ref.shape[1], chunk):
        tail_ref[:, c:c + chunk] = _dot(xn, w_ref[:, n_head + c:n_head + c + chunk]).astype(BF16)


def norm_proj(x, nw, w, *, tm, n_f32, chunk=512):
    t, d = x.shape
    n = w.shape[1]
    row = lambda i: (i, 0)
    return pl.pallas_call(
        functools.partial(_norm_proj_kernel, chunk=chunk),
        grid=(t // tm,),
        in_specs=[pl.BlockSpec((tm, d), row), _resident((1, d)), _resident(w.shape)],
        out_specs=[pl.BlockSpec((tm, n_f32), row), pl.BlockSpec((tm, n - n_f32), row)],
        out_shape=[jax.ShapeDtypeStruct((t, n_f32), F32), jax.ShapeDtypeStruct((t, n - n_f32), BF16)],
        compiler_params=_cparams(1),
        name="norm_proj",
    )(x, nw, w)


def _ffn_kernel(*refs, chunk, final_norm, has_mixer_out, n_side):
    refs = list(refs)
    h_scr = refs.pop()
    side_out = [refs.pop() for _ in range(n_side)][::-1]
    o_ref = refs.pop()
    side_in = [refs.pop() for _ in range(n_side)][::-1]
    _side_cast(side_in, side_out)
    if has_mixer_out:
        y_ref, wo_ref, x_ref, nw_ref, w1_ref, w2_ref, fw_ref = refs
        x = x_ref[...] + _dot(y_ref[...], wo_ref[...])
    else:
        x_ref, nw_ref, w1_ref, w2_ref, fw_ref = refs
        x = x_ref[...]
    xn = _rms(x, nw_ref[...]).astype(BF16)
    for c in range(0, h_scr.shape[1], chunk):
        h = jnp.maximum(_dot(xn, w1_ref[:, c:c + chunk]), 0.0)
        h_scr[:, c:c + chunk] = (h * h).astype(BF16)
    y = x + _dot(h_scr[...], w2_ref[...])
    if final_norm:
        y = _rms(y, fw_ref[...])
    o_ref[...] = y


def ffn(x, nw, w1, w2, fw, *, tm, final_norm, mixer_out=None, side=(), chunk=1024):
    t, d = x.shape
    dff = w1.shape[1]
    row = lambda i: (i, 0)
    args, specs = [], []
    if mixer_out is not None:
        y, wo = mixer_out
        args += [y, wo]
        specs += [pl.BlockSpec((tm, y.shape[1]), row), _resident(wo.shape)]
    args += [x, nw, w1, w2, fw]
    specs += [pl.BlockSpec((tm, d), row), _resident((1, d)), _resident(w1.shape),
              _resident(w2.shape), _resident((1, d))]
    side_in, side_out, side_shapes = _side_cast_specs(side, t // tm, lambda i: i)
    out = pl.pallas_call(
        functools.partial(_ffn_kernel, chunk=chunk, final_norm=final_norm,
                          has_mixer_out=mixer_out is not None, n_side=len(side)),
        grid=(t // tm,),
        in_specs=specs + side_in,
        out_specs=[pl.BlockSpec((tm, d), row)] + side_out,
        out_shape=[jax.ShapeDtypeStruct((t, d), F32)] + side_shapes,
        scratch_shapes=[pltpu.VMEM((tm, dff), BF16)],
        compiler_params=_cparams(1),
        name="ffn",
    )(*args, *[stack for stack, _ in side])
    return out[0], out[1:]


def _retention_kernel(x_ref, nw_ref, w_ref, qg_ref, kg_ref, cos_ref, sin_ref,
                      dm_ref, qd_ref, kd_ref, cd_ref, gw_ref, gb_ref, *rest):
    n_side = (len(rest) - 2) // 2
    z_ref, state = rest[n_side], rest[-1]
    _side_cast(rest[:n_side], rest[n_side + 1:-1])

    @pl.when(pl.program_id(1) == 0)
    def _():
        state[...] = jnp.zeros_like(state)

    nseq, heads, dk, dv = state.shape
    d = heads * dk
    half = dk // 2
    blk = x_ref.shape[1]
    xn = _rms(x_ref[...].reshape(nseq * blk, d), nw_ref[...]).astype(BF16)
    cos = jnp.concatenate([cos_ref[...]] * nseq, axis=0)
    sin = jnp.concatenate([sin_ref[...]] * nseq, axis=0)

    def normed_rotated(col, gain_ref, scale):
        y = _rms(_dot(xn, w_ref[:, col:col + dk]), gain_ref[...])
        t1 = y[:, :half]
        t2 = y[:, half:]
        rot = jnp.concatenate([t1 * cos - t2 * sin, t1 * sin + t2 * cos], axis=1)
        return (rot * scale).astype(BF16)

    for h in range(heads):
        q_all = normed_rotated(h * dk, qg_ref, 1.0)
        k_all = normed_rotated(d + h * dk, kg_ref, dk ** -0.5)
        v_all = _dot(xn, w_ref[:, 2 * d + h * dv:2 * d + (h + 1) * dv]).astype(BF16)
        gate_all = _dot(xn, w_ref[:, 4 * d + h * dv:4 * d + (h + 1) * dv])
        for sq in range(nseq):
            rows = slice(sq * blk, (sq + 1) * blk)
            q, k, v, gate = q_all[rows], k_all[rows], v_all[rows], gate_all[rows]
            st = state[sq, h]
            p = (_dot_nt(q, k) * dm_ref[h]).astype(BF16)
            y = _dot(p, v) + qd_ref[h] * _dot(q, st.astype(BF16))
            kk = (k.astype(F32) * kd_ref[h]).astype(BF16)
            state[sq, h] = st * cd_ref[h] + _dot_tn(kk, v)

            mu = jnp.mean(y, axis=-1, keepdims=True)
            yc = y - mu
            var = jnp.mean(yc * yc, axis=-1, keepdims=True)
            yn = (yc * lax.rsqrt(var + EPS) * gw_ref[:, h * dv:(h + 1) * dv]
                  + gb_ref[:, h * dv:(h + 1) * dv])
            z_ref[sq, :, h * dv:(h + 1) * dv] = (gate * jax.nn.sigmoid(gate) * yn).astype(BF16)


def _ret_decay_tables(blk):
    h = jnp.arange(RET_HEADS, dtype=F32)
    log_g = jnp.log(1.0 - jnp.exp2(-5.0 - h))
    idx = jnp.arange(blk, dtype=F32)
    dist = idx[:, None] - idx[None, :]
    ct = jnp.arange(blk)[:, None] // CHUNK
    cs = jnp.arange(blk)[None, :] // CHUNK
    expo = jnp.where(ct == cs, jnp.abs(dist), dist)
    dm = jnp.where((cs <= ct)[None], jnp.exp(log_g[:, None, None] * expo[None]), 0.0)
    qd = jnp.exp(log_g[:, None] * (idx + 1.0))[..., None]
    kd = jnp.exp(log_g[:, None] * (blk - 1.0 - idx))[..., None]
    cd = jnp.exp(log_g * blk)[:, None, None]
    return dm, qd, kd, cd


def retention(x, nw, w, qg, kg, cos, sin, gn_w, gn_b, *, blk, side=()):
    b, s, d = x.shape
    dk = d // RET_HEADS
    dv = 2 * d // RET_HEADS
    nblk = s // blk
    nseq = 2 if b % 2 == 0 else 1
    dm, qd, kd, cd = _ret_decay_tables(blk)
    tok = lambda bi, n: (bi, n, 0)
    pos = lambda bi, n: (n, 0)
    side_in, side_out, side_shapes = _side_cast_specs(side, b // nseq * nblk,
                                                      lambda bi, n: bi * nblk + n)
    out = pl.pallas_call(
        _retention_kernel,
        grid=(b // nseq, nblk),
        in_specs=[pl.BlockSpec((nseq, blk, d), tok), _resident((1, d)), _resident(w.shape),
                  _resident(qg.shape), _resident(kg.shape),
                  pl.BlockSpec((blk, cos.shape[1]), pos), pl.BlockSpec((blk, sin.shape[1]), pos),
                  _resident(dm.shape), _resident(qd.shape), _resident(kd.shape), _resident(cd.shape),
                  _resident((1, 2 * d)), _resident((1, 2 * d))] + side_in,
        out_specs=[pl.BlockSpec((nseq, blk, 2 * d), tok)] + side_out,
        out_shape=[jax.ShapeDtypeStruct((b, s, 2 * d), BF16)] + side_shapes,
        scratch_shapes=[pltpu.VMEM((nseq, RET_HEADS, dk, dv), F32)],
        compiler_params=_cparams(2),
        name="retention",
    )(x, nw, w, qg, kg, cos, sin, dm, qd, kd, cd, gn_w, gn_b, *[stack for stack, _ in side])
    return out[0], out[1:]


def _conv_mixer_kernel(x_ref, nw_ref, w1_ref, b1_ref, dw_ref, dwb_ref, lw_ref, lb_ref, w2_ref, b2_ref,
                       o_ref, pad_scr, conv_scr, taps_scr, *, rows, chunk):
    tc = x_ref.shape[1]
    width = dw_ref.shape[0]
    d = conv_scr.shape[1]

    @pl.when(pl.program_id(1) == 0)
    def _():
        pad_scr[0, 0:CONV_HALO, :] = jnp.zeros((CONV_HALO, d), F32)

    @pl.when(pl.program_id(1) > 0)
    def _():
        pad_scr[0, 0:CONV_HALO, :] = pad_scr[0, tc:tc + CONV_HALO, :]

    @pl.when((pl.program_id(0) == 0) & (pl.program_id(1) == 0))
    def _():
        for j in range(width):
            taps_scr[j] = jnp.broadcast_to(dw_ref[j:j + 1, :], taps_scr.shape[1:])

    x = x_ref[0]
    xn = _rms(x, nw_ref[...]).astype(BF16)
    for c in range(0, d, chunk):
        a = _dot(xn, w1_ref[:, c:c + chunk]) + b1_ref[:, c:c + chunk]
        gate = _dot(xn, w1_ref[:, d + c:d + c + chunk]) + b1_ref[:, d + c:d + c + chunk]
        pad_scr[0, CONV_HALO:CONV_HALO + tc, c:c + chunk] = a * jax.nn.sigmoid(gate)
    span = pad_scr.shape[1] - SUBLANES
    for r in range(1, SUBLANES):
        pad_scr[r, 0:span, :] = pad_scr[0, r:r + span, :]
    first = CONV_HALO - (width - 1)
    for r0 in range(0, tc, rows):
        acc = None
        for j in range(width):
            shift, base = (first + j) % SUBLANES, (first + j) // SUBLANES * SUBLANES
            window = pad_scr[shift, r0 + base:r0 + base + rows, :]
            term = taps_scr[j][None] * window.reshape(rows // SUBLANES, SUBLANES, d)
            acc = term if acc is None else acc + term
        conv_scr[r0:r0 + rows, :] = acc.reshape(rows, d)
    hc = conv_scr[...] + dwb_ref[...]
    mu = jnp.mean(hc, axis=-1, keepdims=True)
    cen = hc - mu
    var = jnp.mean(cen * cen, axis=-1, keepdims=True)
    hn = cen * lax.rsqrt(var + EPS) * lw_ref[...] + lb_ref[...]
    act = (hn * jax.nn.sigmoid(hn)).astype(BF16)
    o_ref[0] = x + _dot(act, w2_ref[...]) + b2_ref[...]


def conv_mixer(x, nw, w1, b1, dw_w, dw_b, ln_w, ln_b, w2, b2, *, tc, rows=16, chunk=512):
    b, s, d = x.shape
    assert dw_w.shape[0] - 1 <= CONV_HALO <= tc and s % tc == 0 and tc % rows == 0
    tok = lambda bi, n: (bi, n, 0)
    return pl.pallas_call(
        functools.partial(_conv_mixer_kernel, rows=rows, chunk=chunk),
        grid=(b, s // tc),
        in_specs=[pl.BlockSpec((1, tc, d), tok), _resident((1, d)),
                  _resident(w1.shape), _resident((1, 2 * d)),
                  _resident(dw_w.shape), _resident((1, d)), _resident((1, d)), _resident((1, d)),
                  _resident(w2.shape), _resident((1, d))],
        out_specs=pl.BlockSpec((1, tc, d), tok),
        out_shape=jax.ShapeDtypeStruct((b, s, d), F32),
        scratch_shapes=[pltpu.VMEM((SUBLANES, CONV_HALO + tc, d), F32), pltpu.VMEM((tc, d), F32),
                        pltpu.VMEM((dw_w.shape[0], SUBLANES, d), F32)],
        compiler_params=_cparams(2),
        name="conv_mixer",
    )(x, nw, w1, b1, dw_w, dw_b, ln_w, ln_b, w2, b2)


LOG2E = 1.4426950408889634
SOFTPLUS_CLAMP = 96.0
EXP2_UNDERFLOW = 160.0
BF16_SLACK = 1.0625


def _sb_block(q, kb, u2, carry, causal):
    tk = kb.shape[0]

    def mask_own_rows(t):
        own = jnp.where(causal, t[:tk], 0.0)
        return own if t.shape[0] == tk else jnp.concatenate([own, t[tk:]], axis=0)

    z = _dot_nt(q, kb)
    sp = jnp.maximum(jnp.log(1.0 + jnp.exp2(jnp.minimum(z, SOFTPLUS_CLAMP))) * LOG2E, z)
    if causal is not None:
        sp = mask_own_rows(sp)
    r = _dot(sp.astype(BF16), u2)
    a = jnp.exp2(z - r - jnp.concatenate([carry] * (tk // carry.shape[1]), axis=1))
    if causal is not None:
        a = mask_own_rows(a)
    return a.astype(BF16), jnp.broadcast_to(r[:, 0:1], carry.shape)


def _sb_attn_kernel(q_ref, k_ref, v_ref, qg_ref, kg_ref, u2_ref, o_ref,
                    kn_scr, vn_scr, qn_scr, acc_scr, car_scr, *, dh, tk):
    i = pl.program_id(2)
    tq = q_ref.shape[1]
    nsub = tq // tk
    pair = 128 // dh

    def head_lanes(t, e):
        lane = lax.broadcasted_iota(jnp.int32, t.shape, 1)
        return jnp.where(lane // dh == e, t, 0.0)

    def head_rms(t, e, gain):
        te = head_lanes(t, e)
        ms = jnp.sum(te * te, axis=-1, keepdims=True) * (1.0 / dh)
        return te * lax.rsqrt(ms + EPS) * gain

    @pl.when(i == 0)
    def _():
        for e in range(pair):
            kn_scr[e] = head_rms(k_ref[0], e, kg_ref[...]).astype(BF16)
            vn_scr[e] = head_lanes(v_ref[0].astype(F32), e).astype(BF16)

    for e in range(pair):
        qn_scr[e] = (head_rms(q_ref[0], e, qg_ref[...]) * (LOG2E * dh ** -0.5)).astype(BF16)
    acc_scr[...] = jnp.zeros_like(acc_scr)
    car_scr[...] = jnp.zeros_like(car_scr)

    u2 = u2_ref[...]
    row = lax.broadcasted_iota(jnp.int32, (tk, tk), 0)
    col = lax.broadcasted_iota(jnp.int32, (tk, tk), 1)
    causal = col < row

    def step(block, rows, mask):
        start = pl.multiple_of(block * tk, tk)
        vv = jnp.concatenate([vn_scr[e, pl.ds(start, tk), :] for e in range(pair)], axis=0)
        weights = []
        for e in range(pair):
            a, total = _sb_block(qn_scr[e, rows, :], kn_scr[e, pl.ds(start, tk), :], u2,
                                 car_scr[e, rows, :], mask)
            car_scr[e, rows, :] += total
            weights.append(a)
        acc_scr[rows, :] += _dot(jnp.concatenate(weights, axis=1), vv)

    for c in reversed(range(nsub)):
        step(i * nsub + c, slice(c * tk, min((c + 2) * tk, tq)), causal)

    z_bound = (LOG2E * dh ** 0.5 * BF16_SLACK) * jnp.max(jnp.abs(qg_ref[...])) \
        * jnp.max(jnp.abs(kg_ref[...]))
    dead = z_bound + EXP2_UNDERFLOW

    def older_blocks(first_block, rows, low):
        def alive(state):
            block, low = state
            return jnp.logical_and(block >= 0, low <= dead)

        def body(state):
            block, _ = state
            step(block, rows, None)
            return block - 1, jnp.min(car_scr[:, rows, :])

        lax.while_loop(alive, body, (first_block, low))

    low_first = jnp.min(car_scr[:, 0:tk, :])
    low_rest = jnp.min(car_scr[:, tk:, :]) if nsub > 1 else None
    older_blocks(i * nsub - 1, slice(0, tk), low_first)
    if nsub > 1:
        @pl.when(low_rest <= dead)
        def _():
            def per_group(sub, _):
                rows = pl.ds(pl.multiple_of(sub * tk, tk), tk)
                older_blocks(i * nsub + sub - 2, rows, jnp.min(car_scr[:, rows, :]))
                return 0

            lax.fori_loop(1, nsub, per_group, 0)
    o_ref[0] = acc_scr[...].astype(o_ref.dtype)


def sb_attention(qk, v, qg, kg, *, tq, tk):
    b, s, d = v.shape
    dh = d // SB_HEADS
    pair = 128 // dh
    lane_blocks = d // 128
    idx = jnp.arange(tk)
    u2 = (idx[:, None] >= idx[None, :]).astype(BF16)
    return pl.pallas_call(
        functools.partial(_sb_attn_kernel, dh=dh, tk=tk),
        grid=(b, lane_blocks, s // tq),
        in_specs=[pl.BlockSpec((1, tq, 128), lambda bi, hp, i: (bi, i, hp)),
                  pl.BlockSpec((1, s, 128), lambda bi, hp, i: (bi, 0, lane_blocks + hp)),
                  pl.BlockSpec((1, s, 128), lambda bi, hp, i: (bi, 0, hp)),
                  _resident((1, 128)), _resident((1, 128)), _resident((tk, tk))],
        out_specs=pl.BlockSpec((1, tq, 128), lambda bi, hp, i: (bi, i, hp)),
        out_shape=jax.ShapeDtypeStruct((b, s, d), BF16),
        scratch_shapes=[pltpu.VMEM((pair, s, 128), BF16), pltpu.VMEM((pair, s, 128), BF16),
                        pltpu.VMEM((pair, tq, 128), BF16), pltpu.VMEM((tq, 128), F32),
                        pltpu.VMEM((pair, tq, 128), F32)],
        compiler_params=_cparams(3),
        name="sb_attn",
    )(qk, qk, v, jnp.tile(qg, (1, pair)), jnp.tile(kg, (1, pair)), u2)


def _rope_tables(seq, dk):
    half = dk // 2
    inv_freq = ROPE_BASE ** (-jnp.arange(half, dtype=F32) / half)
    ang = jnp.arange(seq, dtype=F32)[:, None] * inv_freq[None, :]
    return jnp.cos(ang), jnp.sin(ang)


def kernel(x, norm_mix, norm_ffn, ret_w_in, ret_q_norm, ret_k_norm, ret_gn_w, ret_gn_b, ret_w_out,
           conv_pw1_w, conv_pw1_b, conv_dw_w, conv_dw_b, conv_ln_w, conv_ln_b, conv_pw2_w, conv_pw2_b,
           sb_w_in, sb_q_norm, sb_k_norm, sb_w_out, ffn_w1, ffn_w2, final_norm):
    b, s, d = x.shape
    depth = norm_mix.shape[0]
    t = b * s
    tm = min(512, s)
    cos, sin = _rope_tables(s, d // RET_HEADS)
    row = lambda a: a.reshape(1, -1)

    stacks = {"ret_in": ret_w_in, "ret_out": ret_w_out, "conv_in": conv_pw1_w, "conv_out": conv_pw2_w,
              "sb_in": sb_w_in, "sb_out": sb_w_out, "ffn1": ffn_w1, "ffn2": ffn_w2}
    mixer_names = (("ret_in", "ret_out"), ("conv_in", "conv_out"), ("sb_in", "sb_out"))

    def layer_weights(i):
        keys = [(name, i // N_MIXERS) for name in mixer_names[i % N_MIXERS]]
        return keys + [("ffn1", i), ("ffn2", i)]

    def side_job(keys):
        return [(stacks[name], idx) for name, idx in keys]

    first, *rest_of_layer0 = layer_weights(0)
    bf16_w = {first: stacks[first[0]][first[1]].astype(BF16)}

    x2 = x.reshape(t, d)
    for i in range(depth):
        kind = i % N_MIXERS
        j = i // N_MIXERS
        nw = row(norm_mix[i])
        mixer_out = None
        if kind == 0:
            pending = rest_of_layer0 if i == 0 else []
            z, cast = retention(x2.reshape(b, s, d), nw, bf16_w[("ret_in", j)], row(ret_q_norm[j]),
                                row(ret_k_norm[j]), cos, sin, row(ret_gn_w[j]), row(ret_gn_b[j]),
                                blk=min(256, s), side=side_job(pending))
            bf16_w.update(zip(pending, cast))
            mixer_out = (z.reshape(t, 2 * d), bf16_w[("ret_out", j)])
        elif kind == 1:
            x2 = conv_mixer(x2.reshape(b, s, d), nw, bf16_w[("conv_in", j)], row(conv_pw1_b[j]),
                            conv_dw_w[j], row(conv_dw_b[j]), row(conv_ln_w[j]), row(conv_ln_b[j]),
                            bf16_w[("conv_out", j)], row(conv_pw2_b[j]),
                            tc=min(512, s)).reshape(t, d)
        else:
            qk, v = norm_proj(x2, nw, bf16_w[("sb_in", j)], tm=tm, n_f32=2 * d)
            y = sb_attention(qk.reshape(b, s, 2 * d), v.reshape(b, s, d), row(sb_q_norm[j]),
                             row(sb_k_norm[j]), tq=min(4096, s), tk=min(256, s))
            mixer_out = (y.reshape(t, d), bf16_w[("sb_out", j)])
        pending = layer_weights(i + 1) if i + 1 < depth else []
        x2, cast = ffn(x2, row(norm_ffn[i]), bf16_w[("ffn1", i)], bf16_w[("ffn2", i)],
                       row(final_norm), tm=tm, final_norm=(i == depth - 1), mixer_out=mixer_out,
                       side=side_job(pending))
        bf16_w.update(zip(pending, cast))
    return x2.reshape(b, s, d)
```

```python
import functools

import jax
import jax.numpy as jnp
from jax import lax
from jax.experimental import pallas as pl
from jax.experimental.pallas import tpu as pltpu

F32 = jnp.float32
BF16 = jnp.bfloat16
EPS = 1e-6

CHUNK = 64
RET_HEADS = 4
ROPE_BASE = 10000.0
SB_HEADS = 16
N_MIXERS = 3

V7X_VMEM_LIMIT_BYTES = 56 * 1024 * 1024
LANES = 128
SUBLANES = 8
BF16_SUBLANES = 16
CONV_HALO = 32


def _cparams(n_axes):
    return pltpu.CompilerParams(
        dimension_semantics=("arbitrary",) * n_axes,
        vmem_limit_bytes=V7X_VMEM_LIMIT_BYTES)


def _resident(shape):
    zeros = (0,) * len(shape)
    return pl.BlockSpec(shape, lambda *_: zeros, pipeline_mode=pl.Buffered(1))


def _side_cast_specs(side, steps, step_of):
    in_specs, out_specs, out_shapes = [], [], []
    for stack, layer in side:
        _, r, c = stack.shape
        tile, rem = divmod(r, steps)
        assert rem == 0 and tile % BF16_SUBLANES == 0, (stack.shape, steps)
        in_specs.append(pl.BlockSpec((None, tile, c),
                                     lambda *g, layer=layer: (layer, step_of(*g), 0)))
        out_specs.append(pl.BlockSpec((tile, c), lambda *g: (step_of(*g), 0)))
        out_shapes.append(jax.ShapeDtypeStruct((r, c), BF16))
    return in_specs, out_specs, out_shapes


def _side_cast(src_refs, dst_refs):
    for src, dst in zip(src_refs, dst_refs):
        dst[...] = src[...].astype(BF16)


def _rms(x, g):
    return x * lax.rsqrt(jnp.mean(x * x, axis=-1, keepdims=True) + EPS) * g


def _dot(a, b):
    return jnp.dot(a, b, preferred_element_type=F32)


def _dot_nt(a, b):
    return lax.dot_general(a, b, (((1,), (1,)), ((), ())), preferred_element_type=F32)


def _dot_tn(a, b):
    return lax.dot_general(a, b, (((0,), (0,)), ((), ())), preferred_element_type=F32)


def _sb_proj_kernel(x_ref, nw_ref, w_ref, qg_ref, kg_ref, o_ref, *, dh, chunk):
    d = x_ref.shape[1]
    xn = _rms(x_ref[...], nw_ref[...]).astype(BF16)

    def head_normed(y, gain):
        lane = lax.broadcasted_iota(jnp.int32, y.shape, 1)
        sq = y * y
        ms = None
        for e in range(LANES // dh):
            in_head = lane // dh == e
            total = jnp.sum(jnp.where(in_head, sq, 0.0), axis=-1, keepdims=True)
            ms = total if ms is None else jnp.where(in_head, total, ms)
        return y * lax.rsqrt(ms * (1.0 / dh) + EPS) * gain

    for base, gain in ((0, qg_ref[...] * (LOG2E * dh ** -0.5)), (d, kg_ref[...])):
        for c in range(0, d, chunk):
            y = _dot(xn, w_ref[:, base + c:base + c + chunk])
            for l in range(0, chunk, LANES):
                o_ref[:, base + c + l:base + c + l + LANES] = head_normed(
                    y[:, l:l + LANES], gain).astype(BF16)
    for c in range(0, d, chunk):
        o_ref[:, 2 * d + c:2 * d + c + chunk] = _dot(
            xn, w_ref[:, 2 * d + c:2 * d + c + chunk]).astype(BF16)


def sb_proj(x, nw, w, qg, kg, *, dh, tm, chunk=512):
    t, d = x.shape
    n = w.shape[1]
    row = lambda i: (i, 0)
    return pl.pallas_call(
        functools.partial(_sb_proj_kernel, dh=dh, chunk=chunk),
        grid=(t // tm,),
        in_specs=[pl.BlockSpec((tm, d), row), _resident((1, d)), _resident(w.shape),
                  _resident(qg.shape), _resident(kg.shape)],
        out_specs=pl.BlockSpec((tm, n), row),
        out_shape=jax.ShapeDtypeStruct((t, n), BF16),
        compiler_params=_cparams(1),
        name="sb_proj",
    )(x, nw, w, qg, kg)


def _ffn_kernel(*refs, chunk, final_norm, has_mixer_out, n_side):
    refs = list(refs)
    h_scr = refs.pop()
    side_out = [refs.pop() for _ in range(n_side)][::-1]
    o_ref = refs.pop()
    side_in = [refs.pop() for _ in range(n_side)][::-1]
    _side_cast(side_in, side_out)
    if has_mixer_out:
        y_ref, wo_ref, x_ref, nw_ref, w1_ref, w2_ref, fw_ref = refs
        x = x_ref[...] + _dot(y_ref[...], wo_ref[...])
    else:
        x_ref, nw_ref, w1_ref, w2_ref, fw_ref = refs
        x = x_ref[...]
    xn = _rms(x, nw_ref[...]).astype(BF16)
    for c in range(0, h_scr.shape[1], chunk):
        h = jnp.maximum(_dot(xn, w1_ref[:, c:c + chunk]), 0.0)
        h_scr[:, c:c + chunk] = (h * h).astype(BF16)
    y = x + _dot(h_scr[...], w2_ref[...])
    if final_norm:
        y = _rms(y, fw_ref[...])
    o_ref[...] = y


def ffn(x, nw, w1, w2, fw, *, tm, final_norm, mixer_out=None, side=(), chunk=1024):
    t, d = x.shape
    dff = w1.shape[1]
    row = lambda i: (i, 0)
    args, specs = [], []
    if mixer_out is not None:
        y, wo = mixer_out
        args += [y, wo]
        specs += [pl.BlockSpec((tm, y.shape[1]), row), _resident(wo.shape)]
    args += [x, nw, w1, w2, fw]
    specs += [pl.BlockSpec((tm, d), row), _resident((1, d)), _resident(w1.shape),
              _resident(w2.shape), _resident((1, d))]
    side_in, side_out, side_shapes = _side_cast_specs(side, t // tm, lambda i: i)
    out = pl.pallas_call(
        functools.partial(_ffn_kernel, chunk=chunk, final_norm=final_norm,
                          has_mixer_out=mixer_out is not None, n_side=len(side)),
        grid=(t // tm,),
        in_specs=specs + side_in,
        out_specs=[pl.BlockSpec((tm, d), row)] + side_out,
        out_shape=[jax.ShapeDtypeStruct((t, d), F32)] + side_shapes,
        scratch_shapes=[pltpu.VMEM((tm, dff), BF16)],
        compiler_params=_cparams(1),
        name="ffn",
    )(*args, *[stack for stack, _ in side])
    return out[0], out[1:]


def _retention_kernel(x_ref, nw_ref, w_ref, qg_ref, kg_ref, cos_ref, sin_ref,
                      dm_ref, qd_ref, kd_ref, cd_ref, gw_ref, gb_ref, *rest):
    n_side = (len(rest) - 2) // 2
    z_ref, state = rest[n_side], rest[-1]
    _side_cast(rest[:n_side], rest[n_side + 1:-1])

    @pl.when(pl.program_id(1) == 0)
    def _():
        state[...] = jnp.zeros_like(state)

    nseq, heads, dk, dv = state.shape
    d = heads * dk
    half = dk // 2
    blk = x_ref.shape[1]
    xn = _rms(x_ref[...].reshape(nseq * blk, d), nw_ref[...]).astype(BF16)
    cos = jnp.concatenate([cos_ref[...]] * nseq, axis=0)
    sin = jnp.concatenate([sin_ref[...]] * nseq, axis=0)

    def normed_rotated(col, gain_ref, scale):
        y = _rms(_dot(xn, w_ref[:, col:col + dk]), gain_ref[...])
        t1 = y[:, :half]
        t2 = y[:, half:]
        rot = jnp.concatenate([t1 * cos - t2 * sin, t1 * sin + t2 * cos], axis=1)
        return (rot * scale).astype(BF16)

    for h in range(heads):
        q_all = normed_rotated(h * dk, qg_ref, 1.0)
        k_all = normed_rotated(d + h * dk, kg_ref, dk ** -0.5)
        v_all = _dot(xn, w_ref[:, 2 * d + h * dv:2 * d + (h + 1) * dv]).astype(BF16)
        gate_all = _dot(xn, w_ref[:, 4 * d + h * dv:4 * d + (h + 1) * dv])
        for sq in range(nseq):
            rows = slice(sq * blk, (sq + 1) * blk)
            q, k, v, gate = q_all[rows], k_all[rows], v_all[rows], gate_all[rows]
            st = state[sq, h]
            p = (_dot_nt(q, k) * dm_ref[h]).astype(BF16)
            y = _dot(p, v) + qd_ref[h] * _dot(q, st.astype(BF16))
            kk = (k.astype(F32) * kd_ref[h]).astype(BF16)
            state[sq, h] = st * cd_ref[h] + _dot_tn(kk, v)

            mu = jnp.mean(y, axis=-1, keepdims=True)
            yc = y - mu
            var = jnp.mean(yc * yc, axis=-1, keepdims=True)
            yn = (yc * lax.rsqrt(var + EPS) * gw_ref[:, h * dv:(h + 1) * dv]
                  + gb_ref[:, h * dv:(h + 1) * dv])
            z_ref[sq, :, h * dv:(h + 1) * dv] = (gate * jax.nn.sigmoid(gate) * yn).astype(BF16)


def _ret_decay_tables(blk):
    h = jnp.arange(RET_HEADS, dtype=F32)
    log_g = jnp.log(1.0 - jnp.exp2(-5.0 - h))
    idx = jnp.arange(blk, dtype=F32)
    dist = idx[:, None] - idx[None, :]
    ct = jnp.arange(blk)[:, None] // CHUNK
    cs = jnp.arange(blk)[None, :] // CHUNK
    expo = jnp.where(ct == cs, jnp.abs(dist), dist)
    dm = jnp.where((cs <= ct)[None], jnp.exp(log_g[:, None, None] * expo[None]), 0.0)
    qd = jnp.exp(log_g[:, None] * (idx + 1.0))[..., None]
    kd = jnp.exp(log_g[:, None] * (blk - 1.0 - idx))[..., None]
    cd = jnp.exp(log_g * blk)[:, None, None]
    return dm, qd, kd, cd


def retention(x, nw, w, qg, kg, cos, sin, gn_w, gn_b, *, blk, side=()):
    b, s, d = x.shape
    dk = d // RET_HEADS
    dv = 2 * d // RET_HEADS
    nblk = s // blk
    nseq = 2 if b % 2 == 0 else 1
    dm, qd, kd, cd = _ret_decay_tables(blk)
    tok = lambda bi, n: (bi, n, 0)
    pos = lambda bi, n: (n, 0)
    side_in, side_out, side_shapes = _side_cast_specs(side, b // nseq * nblk,
                                                      lambda bi, n: bi * nblk + n)
    out = pl.pallas_call(
        _retention_kernel,
        grid=(b // nseq, nblk),
        in_specs=[pl.BlockSpec((nseq, blk, d), tok), _resident((1, d)), _resident(w.shape),
                  _resident(qg.shape), _resident(kg.shape),
                  pl.BlockSpec((blk, cos.shape[1]), pos), pl.BlockSpec((blk, sin.shape[1]), pos),
                  _resident(dm.shape), _resident(qd.shape), _resident(kd.shape), _resident(cd.shape),
                  _resident((1, 2 * d)), _resident((1, 2 * d))] + side_in,
        out_specs=[pl.BlockSpec((nseq, blk, 2 * d), tok)] + side_out,
        out_shape=[jax.ShapeDtypeStruct((b, s, 2 * d), BF16)] + side_shapes,
        scratch_shapes=[pltpu.VMEM((nseq, RET_HEADS, dk, dv), F32)],
        compiler_params=_cparams(2),
        name="retention",
    )(x, nw, w, qg, kg, cos, sin, dm, qd, kd, cd, gn_w, gn_b, *[stack for stack, _ in side])
    return out[0], out[1:]


def _conv_mixer_kernel(x_ref, nw_ref, w1_ref, b1_ref, dw_ref, dwb_ref, lw_ref, lb_ref, w2_ref, b2_ref,
                       o_ref, pad_scr, conv_scr, taps_scr, *, rows, chunk):
    tc = x_ref.shape[1]
    width = dw_ref.shape[0]
    d = conv_scr.shape[1]

    @pl.when(pl.program_id(1) == 0)
    def _():
        pad_scr[0, 0:CONV_HALO, :] = jnp.zeros((CONV_HALO, d), F32)

    @pl.when(pl.program_id(1) > 0)
    def _():
        pad_scr[0, 0:CONV_HALO, :] = pad_scr[0, tc:tc + CONV_HALO, :]

    @pl.when((pl.program_id(0) == 0) & (pl.program_id(1) == 0))
    def _():
        for j in range(width):
            taps_scr[j] = jnp.broadcast_to(dw_ref[j:j + 1, :], taps_scr.shape[1:])

    x = x_ref[0]
    xn = _rms(x, nw_ref[...]).astype(BF16)
    for c in range(0, d, chunk):
        a = _dot(xn, w1_ref[:, c:c + chunk]) + b1_ref[:, c:c + chunk]
        gate = _dot(xn, w1_ref[:, d + c:d + c + chunk]) + b1_ref[:, d + c:d + c + chunk]
        pad_scr[0, CONV_HALO:CONV_HALO + tc, c:c + chunk] = a * jax.nn.sigmoid(gate)
    span = pad_scr.shape[1] - SUBLANES
    for r in range(1, SUBLANES):
        pad_scr[r, 0:span, :] = pad_scr[0, r:r + span, :]
    first = CONV_HALO - (width - 1)
    for r0 in range(0, tc, rows):
        acc = None
        for j in range(width):
            shift, base = (first + j) % SUBLANES, (first + j) // SUBLANES * SUBLANES
            window = pad_scr[shift, r0 + base:r0 + base + rows, :]
            term = taps_scr[j][None] * window.reshape(rows // SUBLANES, SUBLANES, d)
            acc = term if acc is None else acc + term
        conv_scr[r0:r0 + rows, :] = acc.reshape(rows, d)
    hc = conv_scr[...] + dwb_ref[...]
    mu = jnp.mean(hc, axis=-1, keepdims=True)
    cen = hc - mu
    var = jnp.mean(cen * cen, axis=-1, keepdims=True)
    hn = cen * lax.rsqrt(var + EPS) * lw_ref[...] + lb_ref[...]
    act = (hn * jax.nn.sigmoid(hn)).astype(BF16)
    o_ref[0] = x + _dot(act, w2_ref[...]) + b2_ref[...]


def conv_mixer(x, nw, w1, b1, dw_w, dw_b, ln_w, ln_b, w2, b2, *, tc, rows=16, chunk=512):
    b, s, d = x.shape
    assert dw_w.shape[0] - 1 <= CONV_HALO <= tc and s % tc == 0 and tc % rows == 0
    tok = lambda bi, n: (bi, n, 0)
    return pl.pallas_call(
        functools.partial(_conv_mixer_kernel, rows=rows, chunk=chunk),
        grid=(b, s // tc),
        in_specs=[pl.BlockSpec((1, tc, d), tok), _resident((1, d)),
                  _resident(w1.shape), _resident((1, 2 * d)),
                  _resident(dw_w.shape), _resident((1, d)), _resident((1, d)), _resident((1, d)),
                  _resident(w2.shape), _resident((1, d))],
        out_specs=pl.BlockSpec((1, tc, d), tok),
        out_shape=jax.ShapeDtypeStruct((b, s, d), F32),
        scratch_shapes=[pltpu.VMEM((SUBLANES, CONV_HALO + tc, d), F32), pltpu.VMEM((tc, d), F32),
                        pltpu.VMEM((dw_w.shape[0], SUBLANES, d), F32)],
        compiler_params=_cparams(2),
        name="conv_mixer",
    )(x, nw, w1, b1, dw_w, dw_b, ln_w, ln_b, w2, b2)


LOG2E = 1.4426950408889634
SOFTPLUS_CLAMP = 96.0
EXP2_UNDERFLOW = 160.0
BF16_SLACK = 1.0625


def _sb_block(q, kb, u2, carry, causal):
    tk = kb.shape[0]

    def mask_own_rows(t):
        own = jnp.where(causal, t[:tk], 0.0)
        return own if t.shape[0] == tk else jnp.concatenate([own, t[tk:]], axis=0)

    z = _dot_nt(q, kb)
    sp = jnp.maximum(jnp.log(1.0 + jnp.exp2(jnp.minimum(z, SOFTPLUS_CLAMP))) * LOG2E, z)
    if causal is not None:
        sp = mask_own_rows(sp)
    r = _dot(sp.astype(BF16), u2)
    a = jnp.exp2(z - r - jnp.concatenate([carry] * (tk // carry.shape[1]), axis=1))
    if causal is not None:
        a = mask_own_rows(a)
    return a.astype(BF16), jnp.broadcast_to(r[:, 0:1], carry.shape)


def _sb_attn_kernel(q_ref, k_ref, v_ref, qg_ref, kg_ref, u2_ref, o_ref,
                    vn_scr, qn_scr, acc_scr, car_scr, *, dh, tk):
    i = pl.program_id(2)
    tq = q_ref.shape[1]
    nsub = tq // tk
    pair = LANES // dh

    def head_lanes(t, e):
        lane = lax.broadcasted_iota(jnp.int32, t.shape, 1)
        return jnp.where(lane // dh == e, t.astype(F32), 0.0).astype(BF16)

    @pl.when(i == 0)
    def _():
        for e in range(pair):
            vn_scr[e] = head_lanes(v_ref[0], e)

    for e in range(pair):
        qn_scr[e] = head_lanes(q_ref[0], e)
    acc_scr[...] = jnp.zeros_like(acc_scr)
    car_scr[...] = jnp.zeros_like(car_scr)

    u2 = u2_ref[...]
    row = lax.broadcasted_iota(jnp.int32, (tk, tk), 0)
    col = lax.broadcasted_iota(jnp.int32, (tk, tk), 1)
    causal = col < row

    def step(block, rows, mask):
        start = pl.multiple_of(block * tk, tk)
        kb = k_ref[0, pl.ds(start, tk), :]
        vv = jnp.concatenate([vn_scr[e, pl.ds(start, tk), :] for e in range(pair)], axis=0)
        weights = []
        for e in range(pair):
            a, total = _sb_block(qn_scr[e, rows, :], kb, u2, car_scr[e, rows, :], mask)
            car_scr[e, rows, :] += total
            weights.append(a)
        acc_scr[rows, :] += _dot(jnp.concatenate(weights, axis=1), vv)

    for c in reversed(range(nsub)):
        step(i * nsub + c, slice(c * tk, min((c + 2) * tk, tq)), causal)

    z_bound = (LOG2E * dh ** 0.5 * BF16_SLACK) * jnp.max(jnp.abs(qg_ref[...])) \
        * jnp.max(jnp.abs(kg_ref[...]))
    dead = z_bound + EXP2_UNDERFLOW

    def older_blocks(first_block, rows, low):
        def alive(state):
            block, low = state
            return jnp.logical_and(block >= 0, low <= dead)

        def body(state):
            block, _ = state
            step(block, rows, None)
            return block - 1, jnp.min(car_scr[:, rows, :])

        lax.while_loop(alive, body, (first_block, low))

    low_first = jnp.min(car_scr[:, 0:tk, :])
    low_rest = jnp.min(car_scr[:, tk:, :]) if nsub > 1 else None
    older_blocks(i * nsub - 1, slice(0, tk), low_first)
    if nsub > 1:
        @pl.when(low_rest <= dead)
        def _():
            def per_group(sub, _):
                rows = pl.ds(pl.multiple_of(sub * tk, tk), tk)
                older_blocks(i * nsub + sub - 2, rows, jnp.min(car_scr[:, rows, :]))
                return 0

            lax.fori_loop(1, nsub, per_group, 0)
    o_ref[0] = acc_scr[...].astype(o_ref.dtype)


def sb_attention(qkv, qg, kg, *, tq, tk):
    b, s, d3 = qkv.shape
    d = d3 // 3
    dh = d // SB_HEADS
    pair = LANES // dh
    lane_blocks = d // LANES
    idx = jnp.arange(tk)
    u2 = (idx[:, None] >= idx[None, :]).astype(BF16)
    return pl.pallas_call(
        functools.partial(_sb_attn_kernel, dh=dh, tk=tk),
        grid=(b, lane_blocks, s // tq),
        in_specs=[pl.BlockSpec((1, tq, LANES), lambda bi, hp, i: (bi, i, hp)),
                  pl.BlockSpec((1, s, LANES), lambda bi, hp, i: (bi, 0, lane_blocks + hp)),
                  pl.BlockSpec((1, s, LANES), lambda bi, hp, i: (bi, 0, 2 * lane_blocks + hp)),
                  _resident((1, LANES)), _resident((1, LANES)), _resident((tk, tk))],
        out_specs=pl.BlockSpec((1, tq, LANES), lambda bi, hp, i: (bi, i, hp)),
        out_shape=jax.ShapeDtypeStruct((b, s, d), BF16),
        scratch_shapes=[pltpu.VMEM((pair, s, LANES), BF16),
                        pltpu.VMEM((pair, tq, LANES), BF16), pltpu.VMEM((tq, LANES), F32),
                        pltpu.VMEM((pair, tq, LANES), F32)],
        compiler_params=_cparams(3),
        name="sb_attn",
    )(qkv, qkv, qkv, qg, kg, u2)


def _rope_tables(seq, dk):
    half = dk // 2
    inv_freq = ROPE_BASE ** (-jnp.arange(half, dtype=F32) / half)
    ang = jnp.arange(seq, dtype=F32)[:, None] * inv_freq[None, :]
    return jnp.cos(ang), jnp.sin(ang)


def kernel(x, norm_mix, norm_ffn, ret_w_in, ret_q_norm, ret_k_norm, ret_gn_w, ret_gn_b, ret_w_out,
           conv_pw1_w, conv_pw1_b, conv_dw_w, conv_dw_b, conv_ln_w, conv_ln_b, conv_pw2_w, conv_pw2_b,
           sb_w_in, sb_q_norm, sb_k_norm, sb_w_out, ffn_w1, ffn_w2, final_norm):
    b, s, d = x.shape
    depth = norm_mix.shape[0]
    t = b * s
    tm = min(512, s)
    cos, sin = _rope_tables(s, d // RET_HEADS)
    row = lambda a: a.reshape(1, -1)

    stacks = {"ret_in": ret_w_in, "ret_out": ret_w_out, "conv_in": conv_pw1_w, "conv_out": conv_pw2_w,
              "sb_in": sb_w_in, "sb_out": sb_w_out, "ffn1": ffn_w1, "ffn2": ffn_w2}
    mixer_names = (("ret_in", "ret_out"), ("conv_in", "conv_out"), ("sb_in", "sb_out"))

    def layer_weights(i):
        keys = [(name, i // N_MIXERS) for name in mixer_names[i % N_MIXERS]]
        return keys + [("ffn1", i), ("ffn2", i)]

    def side_job(keys):
        return [(stacks[name], idx) for name, idx in keys]

    first, *rest_of_layer0 = layer_weights(0)
    bf16_w = {first: stacks[first[0]][first[1]].astype(BF16)}

    x2 = x.reshape(t, d)
    for i in range(depth):
        kind = i % N_MIXERS
        j = i // N_MIXERS
        nw = row(norm_mix[i])
        mixer_out = None
        if kind == 0:
            pending = rest_of_layer0 if i == 0 else []
            z, cast = retention(x2.reshape(b, s, d), nw, bf16_w[("ret_in", j)], row(ret_q_norm[j]),
                                row(ret_k_norm[j]), cos, sin, row(ret_gn_w[j]), row(ret_gn_b[j]),
                                blk=min(256, s), side=side_job(pending))
            bf16_w.update(zip(pending, cast))
            mixer_out = (z.reshape(t, 2 * d), bf16_w[("ret_out", j)])
        elif kind == 1:
            x2 = conv_mixer(x2.reshape(b, s, d), nw, bf16_w[("conv_in", j)], row(conv_pw1_b[j]),
                            conv_dw_w[j], row(conv_dw_b[j]), row(conv_ln_w[j]), row(conv_ln_b[j]),
                            bf16_w[("conv_out", j)], row(conv_pw2_b[j]),
                            tc=min(512, s)).reshape(t, d)
        else:
            dh = d // SB_HEADS
            qg = jnp.tile(row(sb_q_norm[j]), (1, LANES // dh))
            kg = jnp.tile(row(sb_k_norm[j]), (1, LANES // dh))
            qkv = sb_proj(x2, nw, bf16_w[("sb_in", j)], qg, kg, dh=dh, tm=tm)
            y = sb_attention(qkv.reshape(b, s, 3 * d), qg, kg, tq=min(4096, s), tk=min(256, s))
            mixer_out = (y.reshape(t, d), bf16_w[("sb_out", j)])
        pending = layer_weights(i + 1) if i + 1 < depth else []
        x2, cast = ffn(x2, row(norm_ffn[i]), bf16_w[("ffn1", i)], bf16_w[("ffn2", i)],
                       row(final_norm), tm=tm, final_norm=(i == depth - 1), mixer_out=mixer_out,
                       side=side_job(pending))
        bf16_w.update(zip(pending, cast))
    return x2.reshape(b, s, d)
```

```python
import functools

import jax
import jax.numpy as jnp
from jax import lax
from jax.experimental import pallas as pl
from jax.experimental.pallas import tpu as pltpu

F32 = jnp.float32
BF16 = jnp.bfloat16
EPS = 1e-6

CHUNK = 64
RET_HEADS = 4
ROPE_BASE = 10000.0
SB_HEADS = 16
N_MIXERS = 3

V7X_VMEM_LIMIT_BYTES = 56 * 1024 * 1024
LANES = 128
SUBLANES = 8
BF16_SUBLANES = 16
CONV_HALO = 32


def _cparams(n_axes):
    return pltpu.CompilerParams(
        dimension_semantics=("arbitrary",) * n_axes,
        vmem_limit_bytes=V7X_VMEM_LIMIT_BYTES)


def _resident(shape):
    zeros = (0,) * len(shape)
    return pl.BlockSpec(shape, lambda *_: zeros, pipeline_mode=pl.Buffered(1))


def _side_cast_specs(side, steps, step_of):
    in_specs, out_specs, out_shapes = [], [], []
    for stack, layer in side:
        _, r, c = stack.shape
        tile, rem = divmod(r, steps)
        assert rem == 0 and tile % BF16_SUBLANES == 0, (stack.shape, steps)
        in_specs.append(pl.BlockSpec((None, tile, c),
                                     lambda *g, layer=layer: (layer, step_of(*g), 0)))
        out_specs.append(pl.BlockSpec((tile, c), lambda *g: (step_of(*g), 0)))
        out_shapes.append(jax.ShapeDtypeStruct((r, c), BF16))
    return in_specs, out_specs, out_shapes


def _side_cast(src_refs, dst_refs):
    for src, dst in zip(src_refs, dst_refs):
        dst[...] = src[...].astype(BF16)


def _rms(x, g):
    return x * lax.rsqrt(jnp.mean(x * x, axis=-1, keepdims=True) + EPS) * g


def _dot(a, b):
    return jnp.dot(a, b, preferred_element_type=F32)


def _dot_nt(a, b):
    return lax.dot_general(a, b, (((1,), (1,)), ((), ())), preferred_element_type=F32)


def _dot_tn(a, b):
    return lax.dot_general(a, b, (((0,), (0,)), ((), ())), preferred_element_type=F32)


def _sb_proj_kernel(x_ref, nw_ref, w_ref, qg_ref, kg_ref, o_ref, *, dh, chunk):
    d = x_ref.shape[1]
    xn = _rms(x_ref[...], nw_ref[...]).astype(BF16)

    def head_normed(y, gain):
        lane = lax.broadcasted_iota(jnp.int32, y.shape, 1)
        sq = y * y
        ms = None
        for e in range(LANES // dh):
            in_head = lane // dh == e
            total = jnp.sum(jnp.where(in_head, sq, 0.0), axis=-1, keepdims=True)
            ms = total if ms is None else jnp.where(in_head, total, ms)
        return y * lax.rsqrt(ms * (1.0 / dh) + EPS) * gain

    for base, gain in ((0, qg_ref[...] * (LOG2E * dh ** -0.5)), (d, kg_ref[...])):
        for c in range(0, d, chunk):
            y = _dot(xn, w_ref[:, base + c:base + c + chunk])
            for l in range(0, chunk, LANES):
                o_ref[:, base + c + l:base + c + l + LANES] = head_normed(
                    y[:, l:l + LANES], gain).astype(BF16)
    for c in range(0, d, chunk):
        o_ref[:, 2 * d + c:2 * d + c + chunk] = _dot(
            xn, w_ref[:, 2 * d + c:2 * d + c + chunk]).astype(BF16)


def sb_proj(x, nw, w, qg, kg, *, dh, tm, chunk=512):
    t, d = x.shape
    n = w.shape[1]
    row = lambda i: (i, 0)
    return pl.pallas_call(
        functools.partial(_sb_proj_kernel, dh=dh, chunk=chunk),
        grid=(t // tm,),
        in_specs=[pl.BlockSpec((tm, d), row), _resident((1, d)), _resident(w.shape),
                  _resident(qg.shape), _resident(kg.shape)],
        out_specs=pl.BlockSpec((tm, n), row),
        out_shape=jax.ShapeDtypeStruct((t, n), BF16),
        compiler_params=_cparams(1),
        name="sb_proj",
    )(x, nw, w, qg, kg)


def _ffn_kernel(*refs, chunk, final_norm, has_mixer_out, n_side):
    refs = list(refs)
    h_scr = refs.pop()
    side_out = [refs.pop() for _ in range(n_side)][::-1]
    o_ref = refs.pop()
    side_in = [refs.pop() for _ in range(n_side)][::-1]
    _side_cast(side_in, side_out)
    if has_mixer_out:
        y_ref, wo_ref, x_ref, nw_ref, w1_ref, w2_ref, fw_ref = refs
        x = x_ref[...] + _dot(y_ref[...], wo_ref[...])
    else:
        x_ref, nw_ref, w1_ref, w2_ref, fw_ref = refs
        x = x_ref[...]
    xn = _rms(x, nw_ref[...]).astype(BF16)
    for c in range(0, h_scr.shape[1], chunk):
        h = jnp.maximum(_dot(xn, w1_ref[:, c:c + chunk]), 0.0)
        h_scr[:, c:c + chunk] = (h * h).astype(BF16)
    y = x + _dot(h_scr[...], w2_ref[...])
    if final_norm:
        y = _rms(y, fw_ref[...])
    o_ref[...] = y


def ffn(x, nw, w1, w2, fw, *, tm, final_norm, mixer_out=None, side=(), chunk=1024):
    t, d = x.shape
    dff = w1.shape[1]
    row = lambda i: (i, 0)
    args, specs = [], []
    if mixer_out is not None:
        y, wo = mixer_out
        args += [y, wo]
        specs += [pl.BlockSpec((tm, y.shape[1]), row), _resident(wo.shape)]
    args += [x, nw, w1, w2, fw]
    specs += [pl.BlockSpec((tm, d), row), _resident((1, d)), _resident(w1.shape),
              _resident(w2.shape), _resident((1, d))]
    side_in, side_out, side_shapes = _side_cast_specs(side, t // tm, lambda i: i)
    out = pl.pallas_call(
        functools.partial(_ffn_kernel, chunk=chunk, final_norm=final_norm,
                          has_mixer_out=mixer_out is not None, n_side=len(side)),
        grid=(t // tm,),
        in_specs=specs + side_in,
        out_specs=[pl.BlockSpec((tm, d), row)] + side_out,
        out_shape=[jax.ShapeDtypeStruct((t, d), F32)] + side_shapes,
        scratch_shapes=[pltpu.VMEM((tm, dff), BF16)],
        compiler_params=_cparams(1),
        name="ffn",
    )(*args, *[stack for stack, _ in side])
    return out[0], out[1:]


def _retention_kernel(x_ref, nw_ref, w_ref, qg_ref, kg_ref, cos_ref, sin_ref,
                      dm_ref, qd_ref, kd_ref, cd_ref, gw_ref, gb_ref, *rest):
    n_side = (len(rest) - 2) // 2
    z_ref, state = rest[n_side], rest[-1]
    _side_cast(rest[:n_side], rest[n_side + 1:-1])

    @pl.when(pl.program_id(1) == 0)
    def _():
        state[...] = jnp.zeros_like(state)

    nseq, heads, dk, dv = state.shape
    d = heads * dk
    half = dk // 2
    blk = x_ref.shape[1]
    xn = _rms(x_ref[...].reshape(nseq * blk, d), nw_ref[...]).astype(BF16)
    cos = jnp.concatenate([cos_ref[...]] * nseq, axis=0)
    sin = jnp.concatenate([sin_ref[...]] * nseq, axis=0)

    def normed_rotated(col, gain_ref, scale):
        y = _rms(_dot(xn, w_ref[:, col:col + dk]), gain_ref[...])
        t1 = y[:, :half]
        t2 = y[:, half:]
        rot = jnp.concatenate([t1 * cos - t2 * sin, t1 * sin + t2 * cos], axis=1)
        return (rot * scale).astype(BF16)

    for h in range(heads):
        q_all = normed_rotated(h * dk, qg_ref, 1.0)
        k_all = normed_rotated(d + h * dk, kg_ref, dk ** -0.5)
        v_all = _dot(xn, w_ref[:, 2 * d + h * dv:2 * d + (h + 1) * dv]).astype(BF16)
        gate_all = _dot(xn, w_ref[:, 4 * d + h * dv:4 * d + (h + 1) * dv])
        for sq in range(nseq):
            rows = slice(sq * blk, (sq + 1) * blk)
            q, k, v, gate = q_all[rows], k_all[rows], v_all[rows], gate_all[rows]
            st = state[sq, h]
            p = (_dot_nt(q, k) * dm_ref[h]).astype(BF16)
            y = _dot(p, v) + qd_ref[h] * _dot(q, st.astype(BF16))
            kk = (k.astype(F32) * kd_ref[h]).astype(BF16)
            state[sq, h] = st * cd_ref[h] + _dot_tn(kk, v)

            mu = jnp.mean(y, axis=-1, keepdims=True)
            yc = y - mu
            var = jnp.mean(yc * yc, axis=-1, keepdims=True)
            yn = (yc * lax.rsqrt(var + EPS) * gw_ref[:, h * dv:(h + 1) * dv]
                  + gb_ref[:, h * dv:(h + 1) * dv])
            z_ref[sq, :, h * dv:(h + 1) * dv] = (gate * jax.nn.sigmoid(gate) * yn).astype(BF16)


def _ret_decay_tables(blk):
    h = jnp.arange(RET_HEADS, dtype=F32)
    log_g = jnp.log(1.0 - jnp.exp2(-5.0 - h))
    idx = jnp.arange(blk, dtype=F32)
    dist = idx[:, None] - idx[None, :]
    ct = jnp.arange(blk)[:, None] // CHUNK
    cs = jnp.arange(blk)[None, :] // CHUNK
    expo = jnp.where(ct == cs, jnp.abs(dist), dist)
    dm = jnp.where((cs <= ct)[None], jnp.exp(log_g[:, None, None] * expo[None]), 0.0)
    qd = jnp.exp(log_g[:, None] * (idx + 1.0))[..., None]
    kd = jnp.exp(log_g[:, None] * (blk - 1.0 - idx))[..., None]
    cd = jnp.exp(log_g * blk)[:, None, None]
    return dm, qd, kd, cd


def retention(x, nw, w, qg, kg, cos, sin, gn_w, gn_b, *, blk, side=()):
    b, s, d = x.shape
    dk = d // RET_HEADS
    dv = 2 * d // RET_HEADS
    nblk = s // blk
    nseq = 2 if b % 2 == 0 else 1
    dm, qd, kd, cd = _ret_decay_tables(blk)
    tok = lambda bi, n: (bi, n, 0)
    pos = lambda bi, n: (n, 0)
    side_in, side_out, side_shapes = _side_cast_specs(side, b // nseq * nblk,
                                                      lambda bi, n: bi * nblk + n)
    out = pl.pallas_call(
        _retention_kernel,
        grid=(b // nseq, nblk),
        in_specs=[pl.BlockSpec((nseq, blk, d), tok), _resident((1, d)), _resident(w.shape),
                  _resident(qg.shape), _resident(kg.shape),
                  pl.BlockSpec((blk, cos.shape[1]), pos), pl.BlockSpec((blk, sin.shape[1]), pos),
                  _resident(dm.shape), _resident(qd.shape), _resident(kd.shape), _resident(cd.shape),
                  _resident((1, 2 * d)), _resident((1, 2 * d))] + side_in,
        out_specs=[pl.BlockSpec((nseq, blk, 2 * d), tok)] + side_out,
        out_shape=[jax.ShapeDtypeStruct((b, s, 2 * d), BF16)] + side_shapes,
        scratch_shapes=[pltpu.VMEM((nseq, RET_HEADS, dk, dv), F32)],
        compiler_params=_cparams(2),
        name="retention",
    )(x, nw, w, qg, kg, cos, sin, dm, qd, kd, cd, gn_w, gn_b, *[stack for stack, _ in side])
    return out[0], out[1:]


def _conv_mixer_kernel(x_ref, nw_ref, w1_ref, b1_ref, dw_ref, dwb_ref, lw_ref, lb_ref, w2_ref, b2_ref,
                       o_ref, pad_scr, conv_scr, taps_scr, *, rows, chunk):
    tc = x_ref.shape[1]
    width = dw_ref.shape[0]
    d = conv_scr.shape[1]

    @pl.when(pl.program_id(1) == 0)
    def _():
        pad_scr[0, 0:CONV_HALO, :] = jnp.zeros((CONV_HALO, d), F32)

    @pl.when(pl.program_id(1) > 0)
    def _():
        pad_scr[0, 0:CONV_HALO, :] = pad_scr[0, tc:tc + CONV_HALO, :]

    @pl.when((pl.program_id(0) == 0) & (pl.program_id(1) == 0))
    def _():
        for j in range(width):
            taps_scr[j] = jnp.broadcast_to(dw_ref[j:j + 1, :], taps_scr.shape[1:])

    x = x_ref[0]
    xn = _rms(x, nw_ref[...]).astype(BF16)
    for c in range(0, d, chunk):
        a = _dot(xn, w1_ref[:, c:c + chunk]) + b1_ref[:, c:c + chunk]
        gate = _dot(xn, w1_ref[:, d + c:d + c + chunk]) + b1_ref[:, d + c:d + c + chunk]
        pad_scr[0, CONV_HALO:CONV_HALO + tc, c:c + chunk] = a * jax.nn.sigmoid(gate)
    span = pad_scr.shape[1] - SUBLANES
    for r in range(1, SUBLANES):
        pad_scr[r, 0:span, :] = pad_scr[0, r:r + span, :]
    first = CONV_HALO - (width - 1)
    for r0 in range(0, tc, rows):
        acc = None
        for j in range(width):
            shift, base = (first + j) % SUBLANES, (first + j) // SUBLANES * SUBLANES
            window = pad_scr[shift, r0 + base:r0 + base + rows, :]
            term = taps_scr[j][None] * window.reshape(rows // SUBLANES, SUBLANES, d)
            acc = term if acc is None else acc + term
        conv_scr[r0:r0 + rows, :] = acc.reshape(rows, d)
    hc = conv_scr[...] + dwb_ref[...]
    mu = jnp.mean(hc, axis=-1, keepdims=True)
    cen = hc - mu
    var = jnp.mean(cen * cen, axis=-1, keepdims=True)
    hn = cen * lax.rsqrt(var + EPS) * lw_ref[...] + lb_ref[...]
    act = (hn * jax.nn.sigmoid(hn)).astype(BF16)
    o_ref[0] = x + _dot(act, w2_ref[...]) + b2_ref[...]


def conv_mixer(x, nw, w1, b1, dw_w, dw_b, ln_w, ln_b, w2, b2, *, tc, rows=16, chunk=512):
    b, s, d = x.shape
    assert dw_w.shape[0] - 1 <= CONV_HALO <= tc and s % tc == 0 and tc % rows == 0
    tok = lambda bi, n: (bi, n, 0)
    return pl.pallas_call(
        functools.partial(_conv_mixer_kernel, rows=rows, chunk=chunk),
        grid=(b, s // tc),
        in_specs=[pl.BlockSpec((1, tc, d), tok), _resident((1, d)),
                  _resident(w1.shape), _resident((1, 2 * d)),
                  _resident(dw_w.shape), _resident((1, d)), _resident((1, d)), _resident((1, d)),
                  _resident(w2.shape), _resident((1, d))],
        out_specs=pl.BlockSpec((1, tc, d), tok),
        out_shape=jax.ShapeDtypeStruct((b, s, d), F32),
        scratch_shapes=[pltpu.VMEM((SUBLANES, CONV_HALO + tc, d), F32), pltpu.VMEM((tc, d), F32),
                        pltpu.VMEM((dw_w.shape[0], SUBLANES, d), F32)],
        compiler_params=_cparams(2),
        name="conv_mixer",
    )(x, nw, w1, b1, dw_w, dw_b, ln_w, ln_b, w2, b2)


LOG2E = 1.4426950408889634
SOFTPLUS_CLAMP = 96.0
EXP2_UNDERFLOW = 160.0
BF16_SLACK = 1.0625
SB_NEAR_BLOCKS = 2


def _sb_block(q, kb, u2, carry, causal):
    tk = kb.shape[0]

    def mask_own_rows(t):
        own = jnp.where(causal, t[:tk], 0.0)
        return own if t.shape[0] == tk else jnp.concatenate([own, t[tk:]], axis=0)

    z = _dot_nt(q, kb)
    sp = jnp.maximum(jnp.log(1.0 + jnp.exp2(jnp.minimum(z, SOFTPLUS_CLAMP))) * LOG2E, z)
    if causal is not None:
        sp = mask_own_rows(sp)
    r = _dot(sp.astype(BF16), u2)
    a = jnp.exp2(z - r - jnp.concatenate([carry] * (tk // carry.shape[1]), axis=1))
    if causal is not None:
        a = mask_own_rows(a)
    return a.astype(BF16), jnp.broadcast_to(r[:, 0:1], carry.shape)


def _sb_attn_kernel(q_ref, k_ref, v_ref, qg_ref, kg_ref, u2_ref, o_ref,
                    vn_scr, qn_scr, acc_scr, car_scr, *, dh, tk):
    i = pl.program_id(2)
    tq = q_ref.shape[1]
    nsub = tq // tk
    pair = LANES // dh

    def head_lanes(t, e):
        lane = lax.broadcasted_iota(jnp.int32, t.shape, 1)
        return jnp.where(lane // dh == e, t.astype(F32), 0.0).astype(BF16)

    @pl.when(i == 0)
    def _():
        for e in range(pair):
            vn_scr[e] = head_lanes(v_ref[0], e)

    for e in range(pair):
        qn_scr[e] = head_lanes(q_ref[0], e)
    acc_scr[...] = jnp.zeros_like(acc_scr)
    car_scr[...] = jnp.zeros_like(car_scr)

    u2 = u2_ref[...]
    row = lax.broadcasted_iota(jnp.int32, (tk, tk), 0)
    col = lax.broadcasted_iota(jnp.int32, (tk, tk), 1)
    causal = col < row

    def step(block, rows, mask):
        start = pl.multiple_of(block * tk, tk)
        kb = k_ref[0, pl.ds(start, tk), :]
        vv = jnp.concatenate([vn_scr[e, pl.ds(start, tk), :] for e in range(pair)], axis=0)
        weights = []
        for e in range(pair):
            a, total = _sb_block(qn_scr[e, rows, :], kb, u2, car_scr[e, rows, :], mask)
            car_scr[e, rows, :] += total
            weights.append(a)
        acc_scr[rows, :] += _dot(jnp.concatenate(weights, axis=1), vv)

    for c in reversed(range(nsub)):
        step(i * nsub + c, slice(c * tk, min((c + 1 + SB_NEAR_BLOCKS) * tk, tq)), causal)

    z_bound = (LOG2E * dh ** 0.5 * BF16_SLACK) * jnp.max(jnp.abs(qg_ref[...])) \
        * jnp.max(jnp.abs(kg_ref[...]))
    dead = z_bound + EXP2_UNDERFLOW

    def older_blocks(first_block, rows, low):
        def alive(state):
            block, low = state
            return jnp.logical_and(block >= 0, low <= dead)

        def body(state):
            block, _ = state
            step(block, rows, None)
            return block - 1, jnp.min(car_scr[:, rows, :])

        lax.while_loop(alive, body, (first_block, low))

    low_first = jnp.min(car_scr[:, 0:tk, :])
    low_rest = jnp.min(car_scr[:, tk:, :]) if nsub > 1 else None
    older_blocks(i * nsub - 1, slice(0, tk), low_first)
    if nsub > 1:
        @pl.when(low_rest <= dead)
        def _():
            def per_group(sub, _):
                rows = pl.ds(pl.multiple_of(sub * tk, tk), tk)
                seen = jnp.minimum(sub, SB_NEAR_BLOCKS)
                older_blocks(i * nsub + sub - seen - 1, rows, jnp.min(car_scr[:, rows, :]))
                return 0

            lax.fori_loop(1, nsub, per_group, 0)
    o_ref[0] = acc_scr[...].astype(o_ref.dtype)


def sb_attention(qkv, qg, kg, *, tq, tk):
    b, s, d3 = qkv.shape
    d = d3 // 3
    dh = d // SB_HEADS
    pair = LANES // dh
    lane_blocks = d // LANES
    idx = jnp.arange(tk)
    u2 = (idx[:, None] >= idx[None, :]).astype(BF16)
    return pl.pallas_call(
        functools.partial(_sb_attn_kernel, dh=dh, tk=tk),
        grid=(b, lane_blocks, s // tq),
        in_specs=[pl.BlockSpec((1, tq, LANES), lambda bi, hp, i: (bi, i, hp)),
                  pl.BlockSpec((1, s, LANES), lambda bi, hp, i: (bi, 0, lane_blocks + hp)),
                  pl.BlockSpec((1, s, LANES), lambda bi, hp, i: (bi, 0, 2 * lane_blocks + hp)),
                  _resident((1, LANES)), _resident((1, LANES)), _resident((tk, tk))],
        out_specs=pl.BlockSpec((1, tq, LANES), lambda bi, hp, i: (bi, i, hp)),
        out_shape=jax.ShapeDtypeStruct((b, s, d), BF16),
        scratch_shapes=[pltpu.VMEM((pair, s, LANES), BF16),
                        pltpu.VMEM((pair, tq, LANES), BF16), pltpu.VMEM((tq, LANES), F32),
                        pltpu.VMEM((pair, tq, LANES), F32)],
        compiler_params=_cparams(3),
        name="sb_attn",
    )(qkv, qkv, qkv, qg, kg, u2)


def _rope_tables(seq, dk):
    half = dk // 2
    inv_freq = ROPE_BASE ** (-jnp.arange(half, dtype=F32) / half)
    ang = jnp.arange(seq, dtype=F32)[:, None] * inv_freq[None, :]
    return jnp.cos(ang), jnp.sin(ang)


def kernel(x, norm_mix, norm_ffn, ret_w_in, ret_q_norm, ret_k_norm, ret_gn_w, ret_gn_b, ret_w_out,
           conv_pw1_w, conv_pw1_b, conv_dw_w, conv_dw_b, conv_ln_w, conv_ln_b, conv_pw2_w, conv_pw2_b,
           sb_w_in, sb_q_norm, sb_k_norm, sb_w_out, ffn_w1, ffn_w2, final_norm):
    b, s, d = x.shape
    depth = norm_mix.shape[0]
    t = b * s
    tm = min(512, s)
    cos, sin = _rope_tables(s, d // RET_HEADS)
    row = lambda a: a.reshape(1, -1)

    stacks = {"ret_in": ret_w_in, "ret_out": ret_w_out, "conv_in": conv_pw1_w, "conv_out": conv_pw2_w,
              "sb_in": sb_w_in, "sb_out": sb_w_out, "ffn1": ffn_w1, "ffn2": ffn_w2}
    mixer_names = (("ret_in", "ret_out"), ("conv_in", "conv_out"), ("sb_in", "sb_out"))

    def layer_weights(i):
        keys = [(name, i // N_MIXERS) for name in mixer_names[i % N_MIXERS]]
        return keys + [("ffn1", i), ("ffn2", i)]

    def side_job(keys):
        return [(stacks[name], idx) for name, idx in keys]

    first, *rest_of_layer0 = layer_weights(0)
    bf16_w = {first: stacks[first[0]][first[1]].astype(BF16)}

    x2 = x.reshape(t, d)
    for i in range(depth):
        kind = i % N_MIXERS
        j = i // N_MIXERS
        nw = row(norm_mix[i])
        mixer_out = None
        if kind == 0:
            pending = rest_of_layer0 if i == 0 else []
            z, cast = retention(x2.reshape(b, s, d), nw, bf16_w[("ret_in", j)], row(ret_q_norm[j]),
                                row(ret_k_norm[j]), cos, sin, row(ret_gn_w[j]), row(ret_gn_b[j]),
                                blk=min(256, s), side=side_job(pending))
            bf16_w.update(zip(pending, cast))
            mixer_out = (z.reshape(t, 2 * d), bf16_w[("ret_out", j)])
        elif kind == 1:
            x2 = conv_mixer(x2.reshape(b, s, d), nw, bf16_w[("conv_in", j)], row(conv_pw1_b[j]),
                            conv_dw_w[j], row(conv_dw_b[j]), row(conv_ln_w[j]), row(conv_ln_b[j]),
                            bf16_w[("conv_out", j)], row(conv_pw2_b[j]),
                            tc=min(512, s)).reshape(t, d)
        else:
            dh = d // SB_HEADS
            qg = jnp.tile(row(sb_q_norm[j]), (1, LANES // dh))
            kg = jnp.tile(row(sb_k_norm[j]), (1, LANES // dh))
            qkv = sb_proj(x2, nw, bf16_w[("sb_in", j)], qg, kg, dh=dh, tm=tm)
            y = sb_attention(qkv.reshape(b, s, 3 * d), qg, kg, tq=min(4096, s), tk=min(256, s))
            mixer_out = (y.reshape(t, d), bf16_w[("sb_out", j)])
        pending = layer_weights(i + 1) if i + 1 < depth else []
        x2, cast = ffn(x2, row(norm_ffn[i]), bf16_w[("ffn1", i)], bf16_w[("ffn2", i)],
                       row(final_norm), tm=tm, final_norm=(i == depth - 1), mixer_out=mixer_out,
                       side=side_job(pending))
        bf16_w.update(zip(pending, cast))
    return x2.reshape(b, s, d)
```

```python
import functools

import jax
import jax.numpy as jnp
from jax import lax
from jax.experimental import pallas as pl
from jax.experimental.pallas import tpu as pltpu

F32 = jnp.float32
BF16 = jnp.bfloat16
EPS = 1e-6

CHUNK = 64
RET_HEADS = 4
ROPE_BASE = 10000.0
SB_HEADS = 16
N_MIXERS = 3

V7X_VMEM_LIMIT_BYTES = 56 * 1024 * 1024
LANES = 128
SUBLANES = 8
BF16_SUBLANES = 16
CONV_HALO = 32


def _cparams(n_axes):
    return pltpu.CompilerParams(
        dimension_semantics=("arbitrary",) * n_axes,
        vmem_limit_bytes=V7X_VMEM_LIMIT_BYTES)


def _resident(shape):
    zeros = (0,) * len(shape)
    return pl.BlockSpec(shape, lambda *_: zeros, pipeline_mode=pl.Buffered(1))


def _side_cast_specs(side, steps, step_of):
    in_specs, out_specs, out_shapes = [], [], []
    for stack, layer in side:
        _, r, c = stack.shape
        tile, rem = divmod(r, steps)
        assert rem == 0 and tile % BF16_SUBLANES == 0, (stack.shape, steps)
        in_specs.append(pl.BlockSpec((None, tile, c),
                                     lambda *g, layer=layer: (layer, step_of(*g), 0)))
        out_specs.append(pl.BlockSpec((tile, c), lambda *g: (step_of(*g), 0)))
        out_shapes.append(jax.ShapeDtypeStruct((r, c), BF16))
    return in_specs, out_specs, out_shapes


def _side_cast(src_refs, dst_refs):
    for src, dst in zip(src_refs, dst_refs):
        dst[...] = src[...].astype(BF16)


def _rms(x, g):
    return x * lax.rsqrt(jnp.mean(x * x, axis=-1, keepdims=True) + EPS) * g


def _dot(a, b):
    return jnp.dot(a, b, preferred_element_type=F32)


def _dot_nt(a, b):
    return lax.dot_general(a, b, (((1,), (1,)), ((), ())), preferred_element_type=F32)


def _dot_tn(a, b):
    return lax.dot_general(a, b, (((0,), (0,)), ((), ())), preferred_element_type=F32)


def _sb_proj_kernel(x_ref, nw_ref, w_ref, qg_ref, kg_ref, o_ref, *, dh, chunk):
    d = x_ref.shape[1]
    xn = _rms(x_ref[...], nw_ref[...]).astype(BF16)

    def head_normed(y, gain):
        lane = lax.broadcasted_iota(jnp.int32, y.shape, 1)
        sq = y * y
        ms = None
        for e in range(LANES // dh):
            in_head = lane // dh == e
            total = jnp.sum(jnp.where(in_head, sq, 0.0), axis=-1, keepdims=True)
            ms = total if ms is None else jnp.where(in_head, total, ms)
        return y * lax.rsqrt(ms * (1.0 / dh) + EPS) * gain

    for base, gain in ((0, qg_ref[...] * (LOG2E * dh ** -0.5)), (d, kg_ref[...])):
        for c in range(0, d, chunk):
            y = _dot(xn, w_ref[:, base + c:base + c + chunk])
            for l in range(0, chunk, LANES):
                o_ref[:, base + c + l:base + c + l + LANES] = head_normed(
                    y[:, l:l + LANES], gain).astype(BF16)
    for c in range(0, d, chunk):
        o_ref[:, 2 * d + c:2 * d + c + chunk] = _dot(
            xn, w_ref[:, 2 * d + c:2 * d + c + chunk]).astype(BF16)


def sb_proj(x, nw, w, qg, kg, *, dh, tm, chunk=512):
    t, d = x.shape
    n = w.shape[1]
    row = lambda i: (i, 0)
    return pl.pallas_call(
        functools.partial(_sb_proj_kernel, dh=dh, chunk=chunk),
        grid=(t // tm,),
        in_specs=[pl.BlockSpec((tm, d), row), _resident((1, d)), _resident(w.shape),
                  _resident(qg.shape), _resident(kg.shape)],
        out_specs=pl.BlockSpec((tm, n), row),
        out_shape=jax.ShapeDtypeStruct((t, n), BF16),
        compiler_params=_cparams(1),
        name="sb_proj",
    )(x, nw, w, qg, kg)


def _ffn_kernel(*refs, chunk, final_norm, has_mixer_out, n_side):
    refs = list(refs)
    h_scr = refs.pop()
    side_out = [refs.pop() for _ in range(n_side)][::-1]
    o_ref = refs.pop()
    side_in = [refs.pop() for _ in range(n_side)][::-1]
    _side_cast(side_in, side_out)
    if has_mixer_out:
        y_ref, wo_ref, x_ref, nw_ref, w1_ref, w2_ref, fw_ref = refs
        x = x_ref[...] + _dot(y_ref[...], wo_ref[...])
    else:
        x_ref, nw_ref, w1_ref, w2_ref, fw_ref = refs
        x = x_ref[...]
    xn = _rms(x, nw_ref[...]).astype(BF16)
    for c in range(0, h_scr.shape[1], chunk):
        h = jnp.maximum(_dot(xn, w1_ref[:, c:c + chunk]), 0.0)
        h_scr[:, c:c + chunk] = (h * h).astype(BF16)
    y = x + _dot(h_scr[...], w2_ref[...])
    if final_norm:
        y = _rms(y, fw_ref[...])
    o_ref[...] = y


def ffn(x, nw, w1, w2, fw, *, tm, final_norm, mixer_out=None, side=(), chunk=1024):
    t, d = x.shape
    dff = w1.shape[1]
    row = lambda i: (i, 0)
    args, specs = [], []
    if mixer_out is not None:
        y, wo = mixer_out
        args += [y, wo]
        specs += [pl.BlockSpec((tm, y.shape[1]), row), _resident(wo.shape)]
    args += [x, nw, w1, w2, fw]
    specs += [pl.BlockSpec((tm, d), row), _resident((1, d)), _resident(w1.shape),
              _resident(w2.shape), _resident((1, d))]
    side_in, side_out, side_shapes = _side_cast_specs(side, t // tm, lambda i: i)
    out = pl.pallas_call(
        functools.partial(_ffn_kernel, chunk=chunk, final_norm=final_norm,
                          has_mixer_out=mixer_out is not None, n_side=len(side)),
        grid=(t // tm,),
        in_specs=specs + side_in,
        out_specs=[pl.BlockSpec((tm, d), row)] + side_out,
        out_shape=[jax.ShapeDtypeStruct((t, d), F32)] + side_shapes,
        scratch_shapes=[pltpu.VMEM((tm, dff), BF16)],
        compiler_params=_cparams(1),
        name="ffn",
    )(*args, *[stack for stack, _ in side])
    return out[0], out[1:]


def _retention_kernel(x_ref, nw_ref, w_ref, qg_ref, kg_ref, cos_ref, sin_ref,
                      dm_ref, qd_ref, kd_ref, cd_ref, gw_ref, gb_ref, *rest):
    n_side = (len(rest) - 2) // 2
    z_ref, state = rest[n_side], rest[-1]
    _side_cast(rest[:n_side], rest[n_side + 1:-1])

    @pl.when(pl.program_id(1) == 0)
    def _():
        state[...] = jnp.zeros_like(state)

    nseq, heads, dk, dv = state.shape
    d = heads * dk
    half = dk // 2
    blk = x_ref.shape[1]
    xn = _rms(x_ref[...].reshape(nseq * blk, d), nw_ref[...]).astype(BF16)
    cos = jnp.concatenate([cos_ref[...]] * nseq, axis=0)
    sin = jnp.concatenate([sin_ref[...]] * nseq, axis=0)

    def normed_rotated(col, gain_ref, scale):
        y = _rms(_dot(xn, w_ref[:, col:col + dk]), gain_ref[...])
        t1 = y[:, :half]
        t2 = y[:, half:]
        rot = jnp.concatenate([t1 * cos - t2 * sin, t1 * sin + t2 * cos], axis=1)
        return (rot * scale).astype(BF16)

    for h in range(heads):
        q_all = normed_rotated(h * dk, qg_ref, 1.0)
        k_all = normed_rotated(d + h * dk, kg_ref, dk ** -0.5)
        v_all = _dot(xn, w_ref[:, 2 * d + h * dv:2 * d + (h + 1) * dv]).astype(BF16)
        gate_all = _dot(xn, w_ref[:, 4 * d + h * dv:4 * d + (h + 1) * dv])
        for sq in range(nseq):
            rows = slice(sq * blk, (sq + 1) * blk)
            q, k, v, gate = q_all[rows], k_all[rows], v_all[rows], gate_all[rows]
            st = state[sq, h]
            p = (_dot_nt(q, k) * dm_ref[h]).astype(BF16)
            y = _dot(p, v) + qd_ref[h] * _dot(q, st.astype(BF16))
            kk = (k.astype(F32) * kd_ref[h]).astype(BF16)
            state[sq, h] = st * cd_ref[h] + _dot_tn(kk, v)

            mu = jnp.mean(y, axis=-1, keepdims=True)
            yc = y - mu
            var = jnp.mean(yc * yc, axis=-1, keepdims=True)
            yn = (yc * lax.rsqrt(var + EPS) * gw_ref[:, h * dv:(h + 1) * dv]
                  + gb_ref[:, h * dv:(h + 1) * dv])
            z_ref[sq, :, h * dv:(h + 1) * dv] = (gate * jax.nn.sigmoid(gate) * yn).astype(BF16)


def _ret_decay_tables(blk):
    h = jnp.arange(RET_HEADS, dtype=F32)
    log_g = jnp.log(1.0 - jnp.exp2(-5.0 - h))
    idx = jnp.arange(blk, dtype=F32)
    dist = idx[:, None] - idx[None, :]
    ct = jnp.arange(blk)[:, None] // CHUNK
    cs = jnp.arange(blk)[None, :] // CHUNK
    expo = jnp.where(ct == cs, jnp.abs(dist), dist)
    dm = jnp.where((cs <= ct)[None], jnp.exp(log_g[:, None, None] * expo[None]), 0.0)
    qd = jnp.exp(log_g[:, None] * (idx + 1.0))[..., None]
    kd = jnp.exp(log_g[:, None] * (blk - 1.0 - idx))[..., None]
    cd = jnp.exp(log_g * blk)[:, None, None]
    return dm, qd, kd, cd


def retention(x, nw, w, qg, kg, cos, sin, gn_w, gn_b, *, blk, side=()):
    b, s, d = x.shape
    dk = d // RET_HEADS
    dv = 2 * d // RET_HEADS
    nblk = s // blk
    nseq = 2 if b % 2 == 0 else 1
    dm, qd, kd, cd = _ret_decay_tables(blk)
    tok = lambda bi, n: (bi, n, 0)
    pos = lambda bi, n: (n, 0)
    side_in, side_out, side_shapes = _side_cast_specs(side, b // nseq * nblk,
                                                      lambda bi, n: bi * nblk + n)
    out = pl.pallas_call(
        _retention_kernel,
        grid=(b // nseq, nblk),
        in_specs=[pl.BlockSpec((nseq, blk, d), tok), _resident((1, d)), _resident(w.shape),
                  _resident(qg.shape), _resident(kg.shape),
                  pl.BlockSpec((blk, cos.shape[1]), pos), pl.BlockSpec((blk, sin.shape[1]), pos),
                  _resident(dm.shape), _resident(qd.shape), _resident(kd.shape), _resident(cd.shape),
                  _resident((1, 2 * d)), _resident((1, 2 * d))] + side_in,
        out_specs=[pl.BlockSpec((nseq, blk, 2 * d), tok)] + side_out,
        out_shape=[jax.ShapeDtypeStruct((b, s, 2 * d), BF16)] + side_shapes,
        scratch_shapes=[pltpu.VMEM((nseq, RET_HEADS, dk, dv), F32)],
        compiler_params=_cparams(2),
        name="retention",
    )(x, nw, w, qg, kg, cos, sin, dm, qd, kd, cd, gn_w, gn_b, *[stack for stack, _ in side])
    return out[0], out[1:]


def _conv_mixer_kernel(x_ref, nw_ref, w1_ref, b1_ref, dw_ref, dwb_ref, lw_ref, lb_ref, w2_ref, b2_ref,
                       o_ref, pad_scr, conv_scr, taps_scr, *, rows, chunk):
    tc = x_ref.shape[1]
    width = dw_ref.shape[0]
    d = conv_scr.shape[1]

    @pl.when(pl.program_id(1) == 0)
    def _():
        pad_scr[0, 0:CONV_HALO, :] = jnp.zeros((CONV_HALO, d), F32)

    @pl.when(pl.program_id(1) > 0)
    def _():
        pad_scr[0, 0:CONV_HALO, :] = pad_scr[0, tc:tc + CONV_HALO, :]

    @pl.when((pl.program_id(0) == 0) & (pl.program_id(1) == 0))
    def _():
        for j in range(width):
            taps_scr[j] = jnp.broadcast_to(dw_ref[j:j + 1, :], taps_scr.shape[1:])

    x = x_ref[0]
    xn = _rms(x, nw_ref[...]).astype(BF16)
    for c in range(0, d, chunk):
        a = _dot(xn, w1_ref[:, c:c + chunk]) + b1_ref[:, c:c + chunk]
        gate = _dot(xn, w1_ref[:, d + c:d + c + chunk]) + b1_ref[:, d + c:d + c + chunk]
        pad_scr[0, CONV_HALO:CONV_HALO + tc, c:c + chunk] = a * jax.nn.sigmoid(gate)
    span = pad_scr.shape[1] - SUBLANES
    for r in range(1, SUBLANES):
        pad_scr[r, 0:span, :] = pad_scr[0, r:r + span, :]
    first = CONV_HALO - (width - 1)
    for r0 in range(0, tc, rows):
        acc = None
        for j in range(width):
            shift, base = (first + j) % SUBLANES, (first + j) // SUBLANES * SUBLANES
            window = pad_scr[shift, r0 + base:r0 + base + rows, :]
            term = taps_scr[j][None] * window.reshape(rows // SUBLANES, SUBLANES, d)
            acc = term if acc is None else acc + term
        conv_scr[r0:r0 + rows, :] = acc.reshape(rows, d)
    hc = conv_scr[...] + dwb_ref[...]
    mu = jnp.mean(hc, axis=-1, keepdims=True)
    cen = hc - mu
    var = jnp.mean(cen * cen, axis=-1, keepdims=True)
    hn = cen * lax.rsqrt(var + EPS) * lw_ref[...] + lb_ref[...]
    act = (hn * jax.nn.sigmoid(hn)).astype(BF16)
    o_ref[0] = x + _dot(act, w2_ref[...]) + b2_ref[...]


def conv_mixer(x, nw, w1, b1, dw_w, dw_b, ln_w, ln_b, w2, b2, *, tc, rows=16, chunk=512):
    b, s, d = x.shape
    assert dw_w.shape[0] - 1 <= CONV_HALO <= tc and s % tc == 0 and tc % rows == 0
    tok = lambda bi, n: (bi, n, 0)
    return pl.pallas_call(
        functools.partial(_conv_mixer_kernel, rows=rows, chunk=chunk),
        grid=(b, s // tc),
        in_specs=[pl.BlockSpec((1, tc, d), tok), _resident((1, d)),
                  _resident(w1.shape), _resident((1, 2 * d)),
                  _resident(dw_w.shape), _resident((1, d)), _resident((1, d)), _resident((1, d)),
                  _resident(w2.shape), _resident((1, d))],
        out_specs=pl.BlockSpec((1, tc, d), tok),
        out_shape=jax.ShapeDtypeStruct((b, s, d), F32),
        scratch_shapes=[pltpu.VMEM((SUBLANES, CONV_HALO + tc, d), F32), pltpu.VMEM((tc, d), F32),
                        pltpu.VMEM((dw_w.shape[0], SUBLANES, d), F32)],
        compiler_params=_cparams(2),
        name="conv_mixer",
    )(x, nw, w1, b1, dw_w, dw_b, ln_w, ln_b, w2, b2)


LOG2E = 1.4426950408889634
SOFTPLUS_CLAMP = 96.0
EXP2_UNDERFLOW = 160.0
BF16_SLACK = 1.0625
SB_NEAR_BLOCKS = 2


def _sb_block(q, kb, u2, carry, causal):
    tk = kb.shape[0]

    def mask_own_rows(t):
        own = jnp.where(causal, t[:tk], 0.0)
        return own if t.shape[0] == tk else jnp.concatenate([own, t[tk:]], axis=0)

    z = _dot_nt(q, kb)
    sp = jnp.maximum(jnp.log(1.0 + jnp.exp2(jnp.minimum(z, SOFTPLUS_CLAMP))) * LOG2E, z)
    if causal is not None:
        sp = mask_own_rows(sp)
    r = _dot(sp.astype(BF16), u2)
    a = jnp.exp2(z - r - jnp.concatenate([carry] * (tk // carry.shape[1]), axis=1))
    if causal is not None:
        a = mask_own_rows(a)
    return a.astype(BF16), jnp.broadcast_to(r[:, 0:1], carry.shape)


def _sb_attn_kernel(q_ref, k_ref, v_ref, qg_ref, kg_ref, u2_ref, o_ref,
                    vn_scr, qn_scr, acc_scr, car_scr, *, dh, tk):
    i = pl.program_id(2)
    tq = q_ref.shape[1]
    nsub = tq // tk
    pair = LANES // dh

    def head_lanes(t, e):
        lane = lax.broadcasted_iota(jnp.int32, t.shape, 1)
        return jnp.where(lane // dh == e, t.astype(F32), 0.0).astype(BF16)

    @pl.when(i == 0)
    def _():
        for e in range(pair):
            vn_scr[e] = head_lanes(v_ref[0], e)

    for e in range(pair):
        qn_scr[e] = head_lanes(q_ref[0], e)
    acc_scr[...] = jnp.zeros_like(acc_scr)
    car_scr[...] = jnp.zeros_like(car_scr)

    u2 = u2_ref[...]
    row = lax.broadcasted_iota(jnp.int32, (tk, tk), 0)
    col = lax.broadcasted_iota(jnp.int32, (tk, tk), 1)
    causal = col < row

    def step(block, rows, mask):
        start = pl.multiple_of(block * tk, tk)
        kb = k_ref[0, pl.ds(start, tk), :]
        vv = jnp.concatenate([vn_scr[e, pl.ds(start, tk), :] for e in range(pair)], axis=0)
        weights = []
        for e in range(pair):
            a, total = _sb_block(qn_scr[e, rows, :], kb, u2, car_scr[e, rows, :], mask)
            car_scr[e, rows, :] += total
            weights.append(a)
        acc_scr[rows, :] += _dot(jnp.concatenate(weights, axis=1), vv)

    for c in reversed(range(nsub)):
        step(i * nsub + c, slice(c * tk, min((c + 1 + SB_NEAR_BLOCKS) * tk, tq)), causal)

    z_bound = (LOG2E * dh ** 0.5 * BF16_SLACK) * jnp.max(jnp.abs(qg_ref[...])) \
        * jnp.max(jnp.abs(kg_ref[...]))
    dead = z_bound + EXP2_UNDERFLOW

    def older_blocks(first_block, rows, low):
        def alive(state):
            block, low = state
            return jnp.logical_and(block >= 0, low <= dead)

        def body(state):
            block, _ = state
            step(block, rows, None)
            return block - 1, jnp.min(car_scr[:, rows, :])

        lax.while_loop(alive, body, (first_block, low))

    low_first = jnp.min(car_scr[:, 0:tk, :])
    low_rest = jnp.min(car_scr[:, tk:, :]) if nsub > 1 else None
    older_blocks(i * nsub - 1, slice(0, tk), low_first)
    if nsub > 1:
        @pl.when(low_rest <= dead)
        def _():
            def per_group(sub, _):
                rows = pl.ds(pl.multiple_of(sub * tk, tk), tk)
                seen = jnp.minimum(sub, SB_NEAR_BLOCKS)
                older_blocks(i * nsub + sub - seen - 1, rows, jnp.min(car_scr[:, rows, :]))
                return 0

            lax.fori_loop(1, nsub, per_group, 0)
    o_ref[0] = acc_scr[...].astype(o_ref.dtype)


def sb_attention(qkv, qg, kg, *, tq, tk):
    b, s, d3 = qkv.shape
    d = d3 // 3
    dh = d // SB_HEADS
    pair = LANES // dh
    lane_blocks = d // LANES
    idx = jnp.arange(tk)
    u2 = (idx[:, None] >= idx[None, :]).astype(BF16)
    return pl.pallas_call(
        functools.partial(_sb_attn_kernel, dh=dh, tk=tk),
        grid=(b, lane_blocks, s // tq),
        in_specs=[pl.BlockSpec((1, tq, LANES), lambda bi, hp, i: (bi, i, hp)),
                  pl.BlockSpec((1, s, LANES), lambda bi, hp, i: (bi, 0, lane_blocks + hp)),
                  pl.BlockSpec((1, s, LANES), lambda bi, hp, i: (bi, 0, 2 * lane_blocks + hp)),
                  _resident((1, LANES)), _resident((1, LANES)), _resident((tk, tk))],
        out_specs=pl.BlockSpec((1, tq, LANES), lambda bi, hp, i: (bi, i, hp)),
        out_shape=jax.ShapeDtypeStruct((b, s, d), BF16),
        scratch_shapes=[pltpu.VMEM((pair, s, LANES), BF16),
                        pltpu.VMEM((pair, tq, LANES), BF16), pltpu.VMEM((tq, LANES), F32),
                        pltpu.VMEM((pair, tq, LANES), F32)],
        compiler_params=_cparams(3),
        name="sb_attn",
    )(qkv, qkv, qkv, qg, kg, u2)


def _rope_tables(seq, dk):
    half = dk // 2
    inv_freq = ROPE_BASE ** (-jnp.arange(half, dtype=F32) / half)
    ang = jnp.arange(seq, dtype=F32)[:, None] * inv_freq[None, :]
    return jnp.cos(ang), jnp.sin(ang)


def kernel(x, norm_mix, norm_ffn, ret_w_in, ret_q_norm, ret_k_norm, ret_gn_w, ret_gn_b, ret_w_out,
           conv_pw1_w, conv_pw1_b, conv_dw_w, conv_dw_b, conv_ln_w, conv_ln_b, conv_pw2_w, conv_pw2_b,
           sb_w_in, sb_q_norm, sb_k_norm, sb_w_out, ffn_w1, ffn_w2, final_norm):
    b, s, d = x.shape
    depth = norm_mix.shape[0]
    t = b * s
    tm = min(512, s)
    cos, sin = _rope_tables(s, d // RET_HEADS)
    row = lambda a: a.reshape(1, -1)

    stacks = {"ret_in": ret_w_in, "ret_out": ret_w_out, "conv_in": conv_pw1_w, "conv_out": conv_pw2_w,
              "sb_in": sb_w_in, "sb_out": sb_w_out, "ffn1": ffn_w1, "ffn2": ffn_w2}
    mixer_names = (("ret_in", "ret_out"), ("conv_in", "conv_out"), ("sb_in", "sb_out"))

    def layer_weights(i):
        keys = [(name, i // N_MIXERS) for name in mixer_names[i % N_MIXERS]]
        return keys + [("ffn1", i), ("ffn2", i)]

    def side_job(keys):
        return [(stacks[name], idx) for name, idx in keys]

    first, *rest_of_layer0 = layer_weights(0)
    bf16_w = {first: stacks[first[0]][first[1]].astype(BF16)}

    x2 = x.reshape(t, d)
    for i in range(depth):
        kind = i % N_MIXERS
        j = i // N_MIXERS
        nw = row(norm_mix[i])
        mixer_out = None
        if kind == 0:
            pending = rest_of_layer0 if i == 0 else []
            z, cast = retention(x2.reshape(b, s, d), nw, bf16_w[("ret_in", j)], row(ret_q_norm[j]),
                                row(ret_k_norm[j]), cos, sin, row(ret_gn_w[j]), row(ret_gn_b[j]),
                                blk=min(256, s), side=side_job(pending))
            bf16_w.update(zip(pending, cast))
            mixer_out = (z.reshape(t, 2 * d), bf16_w[("ret_out", j)])
        elif kind == 1:
            x2 = conv_mixer(x2.reshape(b, s, d), nw, bf16_w[("conv_in", j)], row(conv_pw1_b[j]),
                            conv_dw_w[j], row(conv_dw_b[j]), row(conv_ln_w[j]), row(conv_ln_b[j]),
                            bf16_w[("conv_out", j)], row(conv_pw2_b[j]),
                            tc=min(512, s)).reshape(t, d)
        else:
            dh = d // SB_HEADS
            qg = jnp.tile(row(sb_q_norm[j]), (1, LANES // dh))
            kg = jnp.tile(row(sb_k_norm[j]), (1, LANES // dh))
            qkv = sb_proj(x2, nw, bf16_w[("sb_in", j)], qg, kg, dh=dh, tm=min(2 * tm, s))
            y = sb_attention(qkv.reshape(b, s, 3 * d), qg, kg, tq=min(4096, s), tk=min(256, s))
            mixer_out = (y.reshape(t, d), bf16_w[("sb_out", j)])
        pending = layer_weights(i + 1) if i + 1 < depth else []
        x2, cast = ffn(x2, row(norm_ffn[i]), bf16_w[("ffn1", i)], bf16_w[("ffn2", i)],
                       row(final_norm), tm=tm if mixer_out else min(2 * tm, s),
                       final_norm=(i == depth - 1), mixer_out=mixer_out, side=side_job(pending))
        bf16_w.update(zip(pending, cast))
    return x2.reshape(b, s, d)
```
